```python
import math
import jax, jax.numpy as jnp
from jax import lax
import numpy as np

D_MODEL = 1024
BATCH = 4
SEQ = 4096
DEPTH = 2

F32 = jnp.float32
EPS = 1e-6
NEG_INF = -1e30
Q_BLOCK = 128
N_MEM = 256
POS_OFFSET_MAX = 1024
S5_WIDTH = 256
S5_GROUP = 16
S5_GROUPS = S5_WIDTH // S5_GROUP
S5_STATE = 64
S5_DT_MIN = 1e-3
S5_DT_MAX = 1e-1
MLA_HEADS = 4
MLA_Q_RANK = 192
MLA_KV_RANK = 128
MLA_NOPE = 64
MLA_ROPE = 32
MLA_V = 64
MLA_WIDTH = MLA_HEADS * MLA_V
ROPE_THETA = 10000.0
LRU_WIDTH = 256
LRU_BLOCKS = 4
LRU_BLOCK = LRU_WIDTH // LRU_BLOCKS
LRU_CONV = 4
LRU_C = 8.0
DIFF_HEADS = 4
DIFF_HEAD_DIM = 32
DIFF_V = 2 * DIFF_HEAD_DIM
DIFF_WIDTH = DIFF_HEADS * DIFF_V
REL_BUCKETS = 32
REL_MAX_DIST = 128
CROSS_HEADS = 4
CROSS_HEAD_DIM = 64
CROSS_WIDTH = CROSS_HEADS * CROSS_HEAD_DIM
DENSE_FF = 2816
N_EXPERTS = 8
TOP_K = 2
EXPERT_FF = 3584
MOE_BLOCK = 256
N_DENSE = (DEPTH + 1) // 2
N_MOE = DEPTH // 2
N_BRANCH = 4
BRANCH_WIDTH = 256
IN_SIZES = (S5_WIDTH, MLA_Q_RANK, MLA_KV_RANK, MLA_ROPE, LRU_WIDTH, LRU_WIDTH,
            DIFF_HEADS * 2 * DIFF_HEAD_DIM, DIFF_HEADS * 2 * DIFF_HEAD_DIM, DIFF_WIDTH,
            N_BRANCH * D_MODEL)
D_IN = sum(IN_SIZES)

kernel_name = 'hybrid_gated_decoder_layers'


def rms_norm(x, g):
    xf = x.astype(F32)
    y = xf * lax.rsqrt(jnp.mean(xf * xf, axis=-1, keepdims=True) + EPS)
    return (y * g.astype(F32)).astype(x.dtype)


def rope(x, pos):
    half = x.shape[-1] // 2
    inv = ROPE_THETA ** (-jnp.arange(half, dtype=F32) / half)
    ang = pos.astype(F32)[:, :, None, None] * inv
    cos, sin = jnp.cos(ang), jnp.sin(ang)
    x1 = x[..., :half].astype(F32)
    x2 = x[..., half:].astype(F32)
    return jnp.concatenate([x1 * cos - x2 * sin, x1 * sin + x2 * cos], axis=-1).astype(x.dtype)


def rel_bucket(dist):
    n = jnp.maximum(dist, 0)
    exact = REL_BUCKETS // 2
    log_ratio = jnp.log(jnp.maximum(n, exact).astype(F32) / exact) / math.log(REL_MAX_DIST / exact)
    large = jnp.minimum(exact + (log_ratio * (REL_BUCKETS - exact)).astype(jnp.int32), REL_BUCKETS - 1)
    return jnp.where(n < exact, n, large)


def to_blocks(a):
    b, s = a.shape[:2]
    a = a.reshape((b, s // Q_BLOCK, Q_BLOCK) + a.shape[2:])
    return jnp.moveaxis(a, 1, 0)


def from_blocks(a):
    a = jnp.moveaxis(a, 0, 1)
    return a.reshape((a.shape[0], a.shape[1] * a.shape[2]) + a.shape[3:])


def _affine_combine(e1, e2):
    a1, b1 = e1
    a2, b2 = e2
    return (a1 * a2, a2 * b1 + b2)


def _complex_affine_combine(e1, e2):
    a1r, a1i, b1r, b1i = e1
    a2r, a2i, b2r, b2i = e2
    return (a2r * a1r - a2i * a1i, a2r * a1i + a2i * a1r,
            a2r * b1r - a2i * b1i + b2r, a2r * b1i + a2i * b1r + b2i)


def s5_branch(u, lam_re, lam_im, log_step, b_re, b_im, c_re, c_im, d_skip, w_glu, b_glu):
    b, s, _ = u.shape
    uf = u.astype(F32)
    ug = uf.reshape(b, s, S5_GROUPS, S5_GROUP)
    dt = jnp.exp(log_step.astype(F32))[:, None]
    lr = lam_re.astype(F32)
    li = lam_im.astype(F32)
    mag = jnp.exp(lr * dt)
    ar = mag * jnp.cos(li * dt)
    ai = mag * jnp.sin(li * dt)
    den = lr * lr + li * li
    zr = ((ar - 1.0) * lr + ai * li) / den
    zi = (ai * lr - (ar - 1.0) * li) / den
    br = b_re.astype(F32)
    bi = b_im.astype(F32)
    bbr = zr[..., None] * br - zi[..., None] * bi
    bbi = zr[..., None] * bi + zi[..., None] * br
    bur = jnp.einsum('bsgc,gpc->bsgp', ug, bbr)
    bui = jnp.einsum('bsgc,gpc->bsgp', ug, bbi)
    shp = bur.shape
    _, _, hr, hi = lax.associative_scan(
        _complex_affine_combine,
        (jnp.broadcast_to(ar, shp), jnp.broadcast_to(ai, shp), bur, bui), axis=1)
    y = (jnp.einsum('bsgp,gcp->bsgc', hr, c_re.astype(F32))
         - jnp.einsum('bsgp,gcp->bsgc', hi, c_im.astype(F32)))
    y = y.reshape(b, s, S5_WIDTH) + d_skip.astype(F32) * uf
    y = jax.nn.gelu(y)
    y = y * jax.nn.sigmoid(jnp.dot(y, w_glu.astype(F32)) + b_glu.astype(F32))
    return y.astype(u.dtype)


def mla_branch(cq, ckv, kpe, pos, g_cq, g_ckv, w_uq, w_ukv, g_qn, g_kn):
    b, s, _ = cq.shape
    q = jnp.dot(rms_norm(cq, g_cq), w_uq).reshape(b, s, MLA_HEADS, MLA_NOPE + MLA_ROPE)
    kv = jnp.dot(rms_norm(ckv, g_ckv), w_ukv).reshape(b, s, MLA_HEADS, MLA_NOPE + MLA_V)
    k_pe = jnp.broadcast_to(kpe[:, :, None, :], (b, s, MLA_HEADS, MLA_ROPE))
    k = jnp.concatenate([kv[..., :MLA_NOPE], k_pe], axis=-1)
    v = kv[..., MLA_NOPE:]
    q = rms_norm(q, g_qn)
    k = rms_norm(k, g_kn)
    q = jnp.concatenate([q[..., :MLA_NOPE], rope(q[..., MLA_NOPE:], pos)], axis=-1) * (MLA_NOPE + MLA_ROPE) ** -0.5
    k = jnp.concatenate([k[..., :MLA_NOPE], rope(k[..., MLA_NOPE:], pos)], axis=-1)

    def one_block(args):
        qb, pb = args
        sc = jnp.einsum('bqhd,bkhd->bhqk', qb, k).astype(F32)
        mask = pb[:, None, :, None] >= pos[:, None, None, :]
        p = jax.nn.softmax(jnp.where(mask, sc, NEG_INF), axis=-1).astype(v.dtype)
        return jnp.einsum('bhqk,bkhd->bqhd', p, v)

    out = from_blocks(lax.map(one_block, (to_blocks(q), to_blocks(pos))))
    return out.reshape(b, s, MLA_WIDTH)


def rglru_branch(xb, gb, conv_w, conv_b, w_r, b_r, w_i, b_i, lam):
    b, s, w = xb.shape
    xp = jnp.pad(xb, ((0, 0), (LRU_CONV - 1, 0), (0, 0)))
    xc = (conv_b + sum(xp[:, k:k + s] * conv_w[k] for k in range(LRU_CONV))).astype(F32)
    xblk = xc.reshape(b, s, LRU_BLOCKS, LRU_BLOCK)
    r = jax.nn.sigmoid(jnp.einsum('bsnc,ncd->bsnd', xblk, w_r.astype(F32)).reshape(b, s, w) + b_r.astype(F32))
    i = jax.nn.sigmoid(jnp.einsum('bsnc,ncd->bsnd', xblk, w_i.astype(F32)).reshape(b, s, w) + b_i.astype(F32))
    log_a = -LRU_C * r * jax.nn.softplus(-lam.astype(F32))
    a = jnp.exp(log_a)
    inp = jnp.sqrt(-jnp.expm1(2.0 * log_a)) * (i * xc)
    _, hseq = lax.associative_scan(_affine_combine, (a, inp), axis=1)
    return (hseq * jax.nn.gelu(gb.astype(F32))).astype(xb.dtype)


def diff_branch(q, k, v, pos, rel_table, g_qn, g_kn, lq1, lk1, lq2, lk2, g_sub, lam_init):
    b, s, _ = q.shape
    q = rms_norm(q.reshape(b, s, DIFF_HEADS, 2, DIFF_HEAD_DIM), g_qn) * DIFF_HEAD_DIM ** -0.5
    k = rms_norm(k.reshape(b, s, DIFF_HEADS, 2, DIFF_HEAD_DIM), g_kn)
    v = v.reshape(b, s, DIFF_HEADS, DIFF_V)
    k1 = k[:, :, :, 0]
    k2 = k[:, :, :, 1]
    lam = (jnp.exp(jnp.sum(lq1.astype(F32) * lk1.astype(F32)))
           - jnp.exp(jnp.sum(lq2.astype(F32) * lk2.astype(F32))) + lam_init)

    def one_block(args):
        qb, pb = args
        dist = pb[:, :, None] - pos[:, None, :]
        bias = jnp.moveaxis(rel_table[rel_bucket(dist)], -1, 1).astype(F32)
        mask = (dist >= 0)[:, None]

        def attn_probs(qq, kk):
            sc = jnp.einsum('bqhd,bkhd->bhqk', qq, kk).astype(F32) + bias
            return jax.nn.softmax(jnp.where(mask, sc, NEG_INF), axis=-1)

        p = attn_probs(qb[:, :, :, 0], k1) - lam * attn_probs(qb[:, :, :, 1], k2)
        return jnp.einsum('bhqk,bkhd->bqhd', p.astype(v.dtype), v)

    out = from_blocks(lax.map(one_block, (to_blocks(q), to_blocks(pos))))
    out = rms_norm(out, g_sub) * (1.0 - lam_init)
    return out.reshape(b, s, DIFF_WIDTH)


def cross_attention(h, m, w_q, w_k, w_v, w_o, g_qn, g_kn):
    b, s, _ = h.shape
    n = m.shape[1]
    q = rms_norm(jnp.dot(h, w_q).reshape(b, s, CROSS_HEADS, CROSS_HEAD_DIM), g_qn) * CROSS_HEAD_DIM ** -0.5
    k = rms_norm(jnp.dot(m, w_k).reshape(b, n, CROSS_HEADS, CROSS_HEAD_DIM), g_kn)
    v = jnp.dot(m, w_v).reshape(b, n, CROSS_HEADS, CROSS_HEAD_DIM)
    p = jax.nn.softmax(jnp.einsum('bqhd,bkhd->bhqk', q, k).astype(F32), axis=-1).astype(v.dtype)
    o = jnp.einsum('bhqk,bkhd->bqhd', p, v).reshape(b, s, CROSS_WIDTH)
    return jnp.dot(o, w_o)


def swiglu(h, w_gate, w_up, w_down):
    return jnp.dot(jax.nn.silu(jnp.dot(h, w_gate)) * jnp.dot(h, w_up), w_down)


def moe_swiglu(h, w_router, w_gate, w_up, w_down):
    b, s, d = h.shape
    n_tok = b * s
    n_assign = n_tok * TOP_K
    n_blk = -(-(n_assign + N_EXPERTS * (MOE_BLOCK - 1)) // MOE_BLOCK)
    n_rows = n_blk * MOE_BLOCK
    ht = h.reshape(n_tok, d)
    logits = jnp.dot(ht, w_router).astype(F32)
    top_v, top_e = lax.top_k(logits, TOP_K)
    top_w = jax.nn.softmax(top_v, axis=-1)
    flat_e = top_e.reshape(-1)
    flat_t = jnp.repeat(jnp.arange(n_tok, dtype=jnp.int32), TOP_K)
    flat_w = top_w.reshape(-1)
    order = jnp.argsort(flat_e)
    se, st, sw = flat_e[order], flat_t[order], flat_w[order]
    counts = jnp.bincount(flat_e, length=N_EXPERTS)
    padded = (counts + MOE_BLOCK - 1) // MOE_BLOCK * MOE_BLOCK
    grp_start = jnp.cumsum(counts) - counts
    pad_end = jnp.cumsum(padded)
    pad_start = pad_end - padded
    dest = pad_start[se] + jnp.arange(n_assign, dtype=jnp.int32) - grp_start[se]
    row_tok = jnp.zeros((n_rows,), jnp.int32).at[dest].set(st)
    row_w = jnp.zeros((n_rows,), F32).at[dest].set(sw)
    blk_e = jnp.minimum(jnp.searchsorted(pad_end, jnp.arange(n_blk) * MOE_BLOCK, side='right'), N_EXPERTS - 1)
    xs = ht[row_tok].reshape(n_blk, MOE_BLOCK, d)

    def one_block(args):
        xb, e = args
        return swiglu(xb, w_gate[e], w_up[e], w_down[e])

    ys = lax.map(one_block, (xs, blk_e)).reshape(n_rows, d)
    out = jnp.zeros((n_tok, d), h.dtype).at[row_tok].add(ys * row_w[:, None].astype(ys.dtype))
    return out.reshape(b, s, d)


def _in_split_points():
    pts, acc = [], 0
    for w in IN_SIZES[:-1]:
        acc += w
        pts.append(acc)
    return pts


def setup_inputs(seed: int = 0) -> dict:
    key = jax.random.key(seed)
    keys = iter(jax.random.split(key, 80))

    def nrm(shape, scale):
        return scale * jax.random.normal(next(keys), shape, F32)

    def gain(shape):
        return 1.0 + 0.01 * jax.random.normal(next(keys), shape, F32)

    L = DEPTH
    x = nrm((BATCH, SEQ, D_MODEL), 1.0)
    mem = nrm((BATCH, N_MEM, D_MODEL), 1.0)
    offsets = jax.random.randint(next(keys), (BATCH, 1), 0, POS_OFFSET_MAX, jnp.int32)
    positions = offsets + jnp.arange(SEQ, dtype=jnp.int32)[None, :]
    rel_table = nrm((REL_BUCKETS, DIFF_HEADS), 0.5)
    g_mix = gain((L, D_MODEL))
    g_cross = gain((L, D_MODEL))
    g_mem = gain((L, D_MODEL))
    g_ffn = gain((L, D_MODEL))
    w_in = nrm((L, D_MODEL, D_IN), D_MODEL ** -0.5)
    b_gate = nrm((L, N_BRANCH, D_MODEL), 0.01)
    s5_lam_re = -0.5 + nrm((L, S5_GROUPS, S5_STATE), 0.01)
    s5_lam_im = math.pi * jnp.arange(S5_STATE, dtype=F32) + nrm((L, S5_GROUPS, S5_STATE), 0.01)
    s5_log_step = jax.random.uniform(next(keys), (L, S5_GROUPS), F32, math.log(S5_DT_MIN), math.log(S5_DT_MAX))
    s5_b_re = nrm((L, S5_GROUPS, S5_STATE, S5_GROUP), S5_GROUP ** -0.5)
    s5_b_im = nrm((L, S5_GROUPS, S5_STATE, S5_GROUP), S5_GROUP ** -0.5)
    s5_c_re = nrm((L, S5_GROUPS, S5_GROUP, S5_STATE), S5_STATE ** -0.5)
    s5_c_im = nrm((L, S5_GROUPS, S5_GROUP, S5_STATE), S5_STATE ** -0.5)
    s5_d = nrm((L, S5_WIDTH), 1.0)
    s5_w_glu = nrm((L, S5_WIDTH, S5_WIDTH), S5_WIDTH ** -0.5)
    s5_b_glu = nrm((L, S5_WIDTH), 0.01)
    mla_g_cq = gain((L, MLA_Q_RANK))
    mla_g_ckv = gain((L, MLA_KV_RANK))
    mla_w_uq = nrm((L, MLA_Q_RANK, MLA_HEADS * (MLA_NOPE + MLA_ROPE)), MLA_Q_RANK ** -0.5)
    mla_w_ukv = nrm((L, MLA_KV_RANK, MLA_HEADS * (MLA_NOPE + MLA_V)), MLA_KV_RANK ** -0.5)
    mla_g_qn = gain((L, MLA_NOPE + MLA_ROPE))
    mla_g_kn = gain((L, MLA_NOPE + MLA_ROPE))
    lru_conv_w = nrm((L, LRU_CONV, LRU_WIDTH), LRU_CONV ** -0.5)
    lru_conv_b = nrm((L, LRU_WIDTH), 0.01)
    lru_w_r = nrm((L, LRU_BLOCKS, LRU_BLOCK, LRU_BLOCK), LRU_BLOCK ** -0.5)
    lru_b_r = nrm((L, LRU_WIDTH), 0.01)
    lru_w_i = nrm((L, LRU_BLOCKS, LRU_BLOCK, LRU_BLOCK), LRU_BLOCK ** -0.5)
    lru_b_i = nrm((L, LRU_WIDTH), 0.01)
    a_c = jax.random.uniform(next(keys), (L, LRU_WIDTH), F32, 0.9, 0.999)
    a_base = a_c ** (1.0 / LRU_C)
    lru_lam = jnp.log(a_base) - jnp.log1p(-a_base)
    diff_g_qn = gain((L, DIFF_HEAD_DIM))
    diff_g_kn = gain((L, DIFF_HEAD_DIM))
    diff_lq1 = nrm((L, DIFF_HEAD_DIM), 0.1)
    diff_lk1 = nrm((L, DIFF_HEAD_DIM), 0.1)
    diff_lq2 = nrm((L, DIFF_HEAD_DIM), 0.1)
    diff_lk2 = nrm((L, DIFF_HEAD_DIM), 0.1)
    diff_g_sub = gain((L, DIFF_V))
    w_branch = nrm((L, N_BRANCH, BRANCH_WIDTH, D_MODEL), BRANCH_WIDTH ** -0.5)
    w_out = nrm((L, D_MODEL, D_MODEL), D_MODEL ** -0.5)
    x_wq = nrm((L, D_MODEL, CROSS_WIDTH), D_MODEL ** -0.5)
    x_wk = nrm((L, D_MODEL, CROSS_WIDTH), D_MODEL ** -0.5)
    x_wv = nrm((L, D_MODEL, CROSS_WIDTH), D_MODEL ** -0.5)
    x_wo = nrm((L, CROSS_WIDTH, D_MODEL), CROSS_WIDTH ** -0.5)
    x_g_qn = gain((L, CROSS_HEAD_DIM))
    x_g_kn = gain((L, CROSS_HEAD_DIM))
    ffn_w_gate = nrm((N_DENSE, D_MODEL, DENSE_FF), D_MODEL ** -0.5)
    ffn_w_up = nrm((N_DENSE, D_MODEL, DENSE_FF), D_MODEL ** -0.5)
    ffn_w_down = nrm((N_DENSE, DENSE_FF, D_MODEL), DENSE_FF ** -0.5)
    moe_w_router = nrm((N_MOE, D_MODEL, N_EXPERTS), D_MODEL ** -0.5)
    moe_w_gate = nrm((N_MOE, N_EXPERTS, D_MODEL, EXPERT_FF), D_MODEL ** -0.5)
    moe_w_up = nrm((N_MOE, N_EXPERTS, D_MODEL, EXPERT_FF), D_MODEL ** -0.5)
    moe_w_down = nrm((N_MOE, N_EXPERTS, EXPERT_FF, D_MODEL), EXPERT_FF ** -0.5)
    return {'x': x, 'mem': mem, 'positions': positions, 'rel_table': rel_table,
            'g_mix': g_mix, 'g_cross': g_cross, 'g_mem': g_mem, 'g_ffn': g_ffn,
            'w_in': w_in, 'b_gate': b_gate,
            's5_lam_re': s5_lam_re, 's5_lam_im': s5_lam_im, 's5_log_step': s5_log_step,
            's5_b_re': s5_b_re, 's5_b_im': s5_b_im, 's5_c_re': s5_c_re, 's5_c_im': s5_c_im,
            's5_d': s5_d, 's5_w_glu': s5_w_glu, 's5_b_glu': s5_b_glu,
            'mla_g_cq': mla_g_cq, 'mla_g_ckv': mla_g_ckv, 'mla_w_uq': mla_w_uq, 'mla_w_ukv': mla_w_ukv,
            'mla_g_qn': mla_g_qn, 'mla_g_kn': mla_g_kn,
            'lru_conv_w': lru_conv_w, 'lru_conv_b': lru_conv_b, 'lru_w_r': lru_w_r, 'lru_b_r': lru_b_r,
            'lru_w_i': lru_w_i, 'lru_b_i': lru_b_i, 'lru_lam': lru_lam,
            'diff_g_qn': diff_g_qn, 'diff_g_kn': diff_g_kn, 'diff_lq1': diff_lq1, 'diff_lk1': diff_lk1,
            'diff_lq2': diff_lq2, 'diff_lk2': diff_lk2, 'diff_g_sub': diff_g_sub,
            'w_branch': w_branch, 'w_out': w_out,
            'x_wq': x_wq, 'x_wk': x_wk, 'x_wv': x_wv, 'x_wo': x_wo, 'x_g_qn': x_g_qn, 'x_g_kn': x_g_kn,
            'ffn_w_gate': ffn_w_gate, 'ffn_w_up': ffn_w_up, 'ffn_w_down': ffn_w_down,
            'moe_w_router': moe_w_router, 'moe_w_gate': moe_w_gate, 'moe_w_up': moe_w_up,
            'moe_w_down': moe_w_down}


def reference(x, mem, positions, rel_table, g_mix, g_cross, g_mem, g_ffn, w_in, b_gate,
              s5_lam_re, s5_lam_im, s5_log_step, s5_b_re, s5_b_im, s5_c_re, s5_c_im,
              s5_d, s5_w_glu, s5_b_glu,
              mla_g_cq, mla_g_ckv, mla_w_uq, mla_w_ukv, mla_g_qn, mla_g_kn,
              lru_conv_w, lru_conv_b, lru_w_r, lru_b_r, lru_w_i, lru_b_i, lru_lam,
              diff_g_qn, diff_g_kn, diff_lq1, diff_lk1, diff_lq2, diff_lk2, diff_g_sub,
              w_branch, w_out,
              x_wq, x_wk, x_wv, x_wo, x_g_qn, x_g_kn,
              ffn_w_gate, ffn_w_up, ffn_w_down,
              moe_w_router, moe_w_gate, moe_w_up, moe_w_down):
    b, s, _ = x.shape
    split_points = _in_split_points()
    for l in range(DEPTH):
        h = rms_norm(x, g_mix[l])
        z = jnp.dot(h, w_in[l])
        (u_s5, c_q, c_kv, k_pe, x_lru, gate_lru, q_d, k_d, v_d, z_gate) = jnp.split(z, split_points, axis=-1)
        lam_init = 0.8 - 0.6 * math.exp(-0.3 * l)
        branches = (
            s5_branch(u_s5, s5_lam_re[l], s5_lam_im[l], s5_log_step[l], s5_b_re[l], s5_b_im[l],
                      s5_c_re[l], s5_c_im[l], s5_d[l], s5_w_glu[l], s5_b_glu[l]),
            mla_branch(c_q, c_kv, k_pe, positions, mla_g_cq[l], mla_g_ckv[l], mla_w_uq[l], mla_w_ukv[l],
                       mla_g_qn[l], mla_g_kn[l]),
            rglru_branch(x_lru, gate_lru, lru_conv_w[l], lru_conv_b[l], lru_w_r[l], lru_b_r[l],
                         lru_w_i[l], lru_b_i[l], lru_lam[l]),
            diff_branch(q_d, k_d, v_d, positions, rel_table, diff_g_qn[l], diff_g_kn[l], diff_lq1[l],
                        diff_lk1[l], diff_lq2[l], diff_lk2[l], diff_g_sub[l], lam_init),
        )
        z_gate = z_gate.reshape(b, s, N_BRANCH, D_MODEL)
        merged = sum(jax.nn.sigmoid(z_gate[:, :, n] + b_gate[l, n]) * jnp.dot(y, w_branch[l, n])
                     for n, y in enumerate(branches))
        x = x + jnp.dot(merged, w_out[l])
        x = x + cross_attention(rms_norm(x, g_cross[l]), rms_norm(mem, g_mem[l]),
                                x_wq[l], x_wk[l], x_wv[l], x_wo[l], x_g_qn[l], x_g_kn[l])
        hf = rms_norm(x, g_ffn[l])
        if l % 2 == 0:
            x = x + swiglu(hf, ffn_w_gate[l // 2], ffn_w_up[l // 2], ffn_w_down[l // 2])
        else:
            x = x + moe_swiglu(hf, moe_w_router[l // 2], moe_w_gate[l // 2], moe_w_up[l // 2], moe_w_down[l // 2])
    return x
```

```python
import functools
import math

import numpy as np
import jax
import jax.numpy as jnp
from jax import lax
from jax.experimental import pallas as pl
from jax.experimental.pallas import tpu as pltpu

F32 = jnp.float32
BF16 = jnp.bfloat16
I32 = jnp.int32
EPS = 1e-6
NEG = -1e30
HI = lax.Precision.HIGHEST

LANE = 128
SUB = 8
VMEM_BYTES = 64 * 1024 * 1024

S5_W, S5_GROUP, S5_STATE = 256, 16, 64
S5_GROUPS = S5_W // S5_GROUP
S5_N = S5_GROUPS * S5_STATE
MLA_H, MLA_QR, MLA_KVR, MLA_NOPE, MLA_ROPE, MLA_V = 4, 192, 128, 64, 32, 64
MLA_QK = MLA_NOPE + MLA_ROPE
ROPE_THETA = 10000.0
LRU_W, LRU_BLOCKS, LRU_CONV, LRU_C = 256, 4, 4, 8.0
DIFF_H, DIFF_D = 4, 32
DIFF_V = 2 * DIFF_D
DIFF_W = DIFF_H * DIFF_V
REL_BUCKETS, REL_MAX_DIST = 32, 128
CROSS_H, CROSS_D = 4, 64
CROSS_W = CROSS_H * CROSS_D
N_EXPERTS, TOP_K = 8, 2
N_BRANCH, BRANCH_W = 4, 256
MOE_ROWS = 512


def _cparams(sem, frac=0.75):
    return pltpu.CompilerParams(dimension_semantics=sem, vmem_limit_bytes=int(VMEM_BYTES * frac))


def _tile(n, pref):
    t = min(n, pref)
    while n % t:
        t -= SUB
    return t


def _rms(x, g):
    return x * lax.rsqrt(jnp.mean(x * x, axis=-1, keepdims=True) + EPS) * g


def _full(shape):
    nd = len(shape)
    return pl.BlockSpec(shape, lambda *_: (0,) * nd)


def _dot(a, b):
    return jnp.dot(a, b, preferred_element_type=F32)


def _dot_t(a, b):
    return lax.dot_general(a, b, (((1,), (1,)), ((), ())), preferred_element_type=F32)


def _seg_norm(a, bd, width):
    ss = jnp.dot(a * a, bd, preferred_element_type=F32, precision=HI)
    return a * lax.rsqrt(ss * (1.0 / width) + EPS)


def _inproj_body(x_ref, g_ref, wr_ref, wm_ref, wd_ref, zr_ref, zm_ref, zd_ref):
    h = _rms(x_ref[...], g_ref[...]).astype(BF16)
    zr_ref[...] = _dot(h, wr_ref[...])
    zm_ref[...] = _dot(h, wm_ref[...])
    zd_ref[...] = _dot(h, wd_ref[...])


def _inproj(x, g, w_rec, w_mla, w_diff):
    t, d = x.shape
    tm = _tile(t, 512)
    outs = [jax.ShapeDtypeStruct((t, w.shape[1]), F32) for w in (w_rec, w_mla, w_diff)]
    return pl.pallas_call(
        _inproj_body,
        out_shape=outs,
        grid=(t // tm,),
        in_specs=[pl.BlockSpec((tm, d), lambda i: (i, 0)), _full(g.shape),
                  _full(w_rec.shape), _full(w_mla.shape), _full(w_diff.shape)],
        out_specs=[pl.BlockSpec((tm, w.shape[1]), lambda i: (i, 0)) for w in (w_rec, w_mla, w_diff)],
        compiler_params=_cparams(("parallel",)),
        name="inproj",
    )(x, g, w_rec, w_mla, w_diff)


def _recurrent_body(z_ref, ar_ref, ai_ref, bre_ref, bim_ref, cre_ref, cim_ref, d_ref, wglu_ref, bglu_ref,
                    cw_ref, cb_ref, wr_ref, br_ref, wi_ref, bi_ref, lam_ref,
                    o_ref, bur, bui, st_re, st_im, xbuf, a_scr, h_scr, st_lru, *, steps):
    rows = steps * SUB
    halo = (LRU_CONV - 1) * SUB

    @pl.when(pl.program_id(0) == 0)
    def _init():
        st_re[...] = jnp.zeros(st_re.shape, F32)
        st_im[...] = jnp.zeros(st_im.shape, F32)
        st_lru[...] = jnp.zeros(st_lru.shape, F32)
        xbuf[0:halo, :] = jnp.zeros((halo, LRU_W), F32)

    u = z_ref[:, 0:S5_W]
    ub = u.astype(BF16)
    bur[...] = _dot(ub, bre_ref[...])
    bui[...] = _dot(ub, bim_ref[...])
    ar = jnp.broadcast_to(ar_ref[...], (SUB, S5_N))
    ai = jnp.broadcast_to(ai_ref[...], (SUB, S5_N))

    def s5_step(t, carry):
        hr, hi = carry
        r = pl.multiple_of(t * SUB, SUB)
        nhr = ar * hr - ai * hi + bur[pl.ds(r, SUB), :]
        nhi = ar * hi + ai * hr + bui[pl.ds(r, SUB), :]
        bur[pl.ds(r, SUB), :] = nhr
        bui[pl.ds(r, SUB), :] = nhi
        return nhr, nhi

    hr, hi = lax.fori_loop(0, steps, s5_step, (st_re[...], st_im[...]))
    st_re[...] = hr
    st_im[...] = hi
    y = _dot(bur[...].astype(BF16), cre_ref[...]) + _dot(bui[...].astype(BF16), cim_ref[...]) + d_ref[...] * u
    y = jax.nn.gelu(y)
    y = y * jax.nn.sigmoid(_dot(y.astype(BF16), wglu_ref[...]) + bglu_ref[...])
    o_ref[:, 0:S5_W] = y.astype(o_ref.dtype)

    xbuf[halo:halo + rows, :] = z_ref[:, S5_W:S5_W + LRU_W]
    xc = cb_ref[...] + sum(xbuf[k * SUB:k * SUB + rows, :] * cw_ref[k:k + 1, :] for k in range(LRU_CONV))
    xbuf[0:halo, :] = xbuf[rows:rows + halo, :]
    xcb = xc.astype(BF16)
    rg = jax.nn.sigmoid(_dot(xcb, wr_ref[...]) + br_ref[...])
    ig = jax.nn.sigmoid(_dot(xcb, wi_ref[...]) + bi_ref[...])
    nl = -lam_ref[...]
    softplus = jnp.maximum(nl, 0.0) + jnp.log1p(jnp.exp(-jnp.abs(nl)))
    log_a = -LRU_C * rg * softplus
    a_scr[...] = jnp.exp(log_a)
    th = jnp.tanh(log_a)
    h_scr[...] = jnp.sqrt(-2.0 * th / (1.0 - th)) * (ig * xc)

    def lru_step(t, h):
        r = pl.multiple_of(t * SUB, SUB)
        nh = a_scr[pl.ds(r, SUB), :] * h + h_scr[pl.ds(r, SUB), :]
        h_scr[pl.ds(r, SUB), :] = nh
        return nh

    st_lru[...] = lax.fori_loop(0, steps, lru_step, st_lru[...])
    gate = z_ref[:, S5_W + LRU_W:S5_W + 2 * LRU_W]
    o_ref[:, S5_W:S5_W + LRU_W] = (h_scr[...] * jax.nn.gelu(gate)).astype(o_ref.dtype)


def _recurrent(z, p):
    rows_total = z.shape[0]
    steps = _tile(rows_total // SUB, 64)
    rows = steps * SUB
    halo = (LRU_CONV - 1) * SUB
    params = [p["ar"], p["ai"], p["bre"], p["bim"], p["cre"], p["cim"], p["d"], p["wglu"], p["bglu"],
              p["cw"], p["cb"], p["wr"], p["br"], p["wi"], p["bi"], p["lam"]]
    return pl.pallas_call(
        functools.partial(_recurrent_body, steps=steps),
        out_shape=jax.ShapeDtypeStruct((rows_total, S5_W + LRU_W), BF16),
        grid=(rows_total // rows,),
        in_specs=[pl.BlockSpec((rows, z.shape[1]), lambda i: (i, 0))] + [_full(a.shape) for a in params],
        out_specs=pl.BlockSpec((rows, S5_W + LRU_W), lambda i: (i, 0)),
        scratch_shapes=[pltpu.VMEM((rows, S5_N), F32), pltpu.VMEM((rows, S5_N), F32),
                        pltpu.VMEM((SUB, S5_N), F32), pltpu.VMEM((SUB, S5_N), F32),
                        pltpu.VMEM((rows + halo, LRU_W), F32), pltpu.VMEM((rows, LRU_W), F32),
                        pltpu.VMEM((rows, LRU_W), F32), pltpu.VMEM((SUB, LRU_W), F32)],
        compiler_params=_cparams(("arbitrary",)),
        name="recurrent",
    )(z, *params)


def _mla_prep_body(z_ref, pos_ref, gcq_ref, gckv_ref, wq_ref, wqs_ref, wk_ref, wv_ref,
                   gq_ref, gqs_ref, gk_ref, gks_ref, inv_ref, sgn_ref, q_ref, k_ref, v_ref):
    cq = z_ref[:, 0:2 * LANE]
    ckv = z_ref[:, 2 * LANE:3 * LANE]
    kpe = z_ref[:, 3 * LANE:4 * LANE]
    kpe_sw = z_ref[:, 4 * LANE:5 * LANE]
    hq = (cq * lax.rsqrt(jnp.sum(cq * cq, axis=-1, keepdims=True) * (1.0 / MLA_QR) + EPS) * gcq_ref[...]).astype(BF16)
    hkv = _rms(ckv, gckv_ref[...]).astype(BF16)
    q = _dot(hq, wq_ref[...])
    qs = _dot(hq, wqs_ref[...])
    kn = _dot(hkv, wk_ref[...])
    v_ref[...] = _dot(hkv, wv_ref[...]).astype(v_ref.dtype)
    ang = pos_ref[...].astype(F32) * inv_ref[...]
    cos = jnp.cos(ang)
    sin = jnp.sin(ang) * sgn_ref[...]
    qc, qsn = gq_ref[...] * cos * (MLA_QK ** -0.5), gqs_ref[...] * sin * (MLA_QK ** -0.5)
    kc, ksn = gk_ref[...] * cos, gks_ref[...] * sin
    for h in range(MLA_H):
        sl = slice(h * LANE, (h + 1) * LANE)
        qh = q[:, sl]
        r = lax.rsqrt(jnp.sum(qh * qh, axis=-1, keepdims=True) * (1.0 / MLA_QK) + EPS)
        q_ref[:, sl] = ((qh * qc + qs[:, sl] * qsn) * r).astype(q_ref.dtype)
        kh = kn[:, sl] + kpe
        r = lax.rsqrt(jnp.sum(kh * kh, axis=-1, keepdims=True) * (1.0 / MLA_QK) + EPS)
        k_ref[:, sl] = ((kh * kc + kpe_sw * ksn) * r).astype(k_ref.dtype)


def _mla_prep(z, pos_col, p):
    t = z.shape[0]
    tm = _tile(t, 512)
    params = [p["gcq"], p["gckv"], p["wq"], p["wqs"], p["wk"], p["wv"],
              p["gq"], p["gqs"], p["gk"], p["gks"], p["inv"], p["sgn"]]
    width = MLA_H * LANE
    return pl.pallas_call(
        _mla_prep_body,
        out_shape=[jax.ShapeDtypeStruct((t, width), BF16)] * 3,
        grid=(t // tm,),
        in_specs=[pl.BlockSpec((tm, z.shape[1]), lambda i: (i, 0)), pl.BlockSpec((tm, 1), lambda i: (i, 0))]
        + [_full(a.shape) for a in params],
        out_specs=[pl.BlockSpec((tm, width), lambda i: (i, 0))] * 3,
        compiler_params=_cparams(("parallel",)),
        name="mla_prep",
    )(z, pos_col, *params)


def _block_tables(positions, blk):
    b, s = positions.shape
    nb = s // blk
    pb = positions.reshape(b, nb, blk)
    bmin = pb.min(-1)
    bmax = pb.max(-1)
    live = bmax[:, :, None] >= bmin[:, None, :]
    last = jnp.max(jnp.where(live, jnp.arange(nb, dtype=I32)[None, None, :], 0), axis=-1)
    return bmin.reshape(-1).astype(I32), bmax.reshape(-1).astype(I32), last.reshape(-1).astype(I32)


def _mla_attn_body(bmin_ref, bmax_ref, last_ref, q_ref, k_ref, v_ref, pq_ref, pk_ref, o_ref,
                   m_scr, l_scr, acc_scr, *, nb):
    b, i, j = pl.program_id(0), pl.program_id(1), pl.program_id(2)

    @pl.when(j == 0)
    def _init():
        m_scr[...] = jnp.full(m_scr.shape, -jnp.inf, F32)
        l_scr[...] = jnp.zeros(l_scr.shape, F32)
        acc_scr[...] = jnp.zeros(acc_scr.shape, F32)

    @pl.when(bmax_ref[b * nb + i] >= bmin_ref[b * nb + j])
    def _block():
        mask = pq_ref[...] >= pk_ref[...]
        for h in range(MLA_H):
            sl = slice(h * LANE, (h + 1) * LANE)
            s = jnp.where(mask, _dot_t(q_ref[:, sl], k_ref[:, sl]), NEG)
            m_prev = m_scr[h]
            m_new = jnp.maximum(m_prev, jnp.max(s, axis=-1, keepdims=True))
            alpha = jnp.exp(m_prev - m_new)
            p = jnp.exp(s - m_new)
            l_scr[h] = alpha * l_scr[h] + jnp.sum(p, axis=-1, keepdims=True)
            acc_scr[:, sl] = alpha * acc_scr[:, sl] + _dot(p.astype(BF16), v_ref[:, sl])
            m_scr[h] = m_new

    @pl.when(j == nb - 1)
    def _fin():
        for h in range(MLA_H):
            sl = slice(h * LANE, (h + 1) * LANE)
            o_ref[:, sl] = (acc_scr[:, sl] / l_scr[h]).astype(o_ref.dtype)


def _mla_attn(q, k, v, positions):
    b, s = positions.shape
    width = q.shape[-1]
    blk = _tile(s, 512)
    nb = s // blk
    bmin, bmax, last = _block_tables(positions, blk)
    q3, k3, v3 = (a.reshape(b, s, width) for a in (q, k, v))
    kv_spec = pl.BlockSpec((None, blk, width), lambda bb, i, j, mn, mx, la: (bb, jnp.minimum(j, la[bb * nb + i]), 0))
    out = pl.pallas_call(
        functools.partial(_mla_attn_body, nb=nb),
        out_shape=jax.ShapeDtypeStruct((b, s, width), BF16),
        grid_spec=pltpu.PrefetchScalarGridSpec(
            num_scalar_prefetch=3,
            grid=(b, nb, nb),
            in_specs=[pl.BlockSpec((None, blk, width), lambda bb, i, j, *_: (bb, i, 0)), kv_spec, kv_spec,
                      pl.BlockSpec((None, blk, 1), lambda bb, i, j, *_: (bb, i, 0)),
                      pl.BlockSpec((None, 1, blk), lambda bb, i, j, mn, mx, la: (bb, 0, jnp.minimum(j, la[bb * nb + i])))],
            out_specs=pl.BlockSpec((None, blk, width), lambda bb, i, j, *_: (bb, i, 0)),
            scratch_shapes=[pltpu.VMEM((MLA_H, blk, 1), F32), pltpu.VMEM((MLA_H, blk, 1), F32),
                            pltpu.VMEM((blk, width), F32)]),
        compiler_params=_cparams(("parallel", "parallel", "arbitrary")),
        name="mla_attn",
    )(bmin, bmax, last, q3, k3, v3, positions.reshape(b, s, 1), positions.reshape(b, 1, s))
    return out.reshape(b * s, width)


def _diff_prep_body(z_ref, gq_ref, gk_ref, bd_ref, q_ref, k_ref, v_ref):
    bd = bd_ref[...]
    q_ref[...] = (_seg_norm(z_ref[:, 0:DIFF_W], bd, DIFF_D) * gq_ref[...]).astype(q_ref.dtype)
    k_ref[...] = (_seg_norm(z_ref[:, DIFF_W:2 * DIFF_W], bd, DIFF_D) * gk_ref[...]).astype(k_ref.dtype)
    v_ref[...] = z_ref[:, 2 * DIFF_W:3 * DIFF_W].astype(v_ref.dtype)


def _diff_prep(z, p):
    t = z.shape[0]
    tm = _tile(t, 512)
    params = [p["gq"], p["gk"], p["bd32"]]
    return pl.pallas_call(
        _diff_prep_body,
        out_shape=[jax.ShapeDtypeStruct((t, DIFF_W), BF16)] * 3,
        grid=(t // tm,),
        in_specs=[pl.BlockSpec((tm, z.shape[1]), lambda i: (i, 0))] + [_full(a.shape) for a in params],
        out_specs=[pl.BlockSpec((tm, DIFF_W), lambda i: (i, 0))] * 3,
        compiler_params=_cparams(("parallel",)),
        name="diff_prep",
    )(z, *params)


def _diff_attn_body(bmin_ref, bmax_ref, last_ref, q_ref, k_ref, v_ref, pq_ref, pk_ref, tab_ref, lqk_ref,
                    gsub_ref, bd_ref, o_ref, qm_scr, m_scr, l_scr, acc1_scr, acc2_scr, *, nb, lam_init):
    b, i, j = pl.program_id(0), pl.program_id(1), pl.program_id(2)
    tq = q_ref.shape[0]
    tk = k_ref.shape[0]
    lane = lax.broadcasted_iota(I32, (1, DIFF_W), 1)
    n_chunks = 2 * DIFF_H

    @pl.when(j == 0)
    def _init():
        m_scr[...] = jnp.full(m_scr.shape, -jnp.inf, F32)
        l_scr[...] = jnp.zeros(l_scr.shape, F32)
        acc1_scr[...] = jnp.zeros(acc1_scr.shape, F32)
        acc2_scr[...] = jnp.zeros(acc2_scr.shape, F32)
        qv = q_ref[...]
        for c in range(n_chunks):
            qm_scr[c] = jnp.where(lane // DIFF_D == c, qv, jnp.zeros_like(qv))

    @pl.when(bmax_ref[b * nb + i] >= bmin_ref[b * nb + j])
    def _block():
        dist = pq_ref[...] - pk_ref[...]
        mask = dist >= 0
        n = jnp.clip(dist, 0, REL_MAX_DIST - 1)
        kb = k_ref[...]
        vb = v_ref[...]
        pv = [jnp.zeros((tq, DIFF_W), F32), jnp.zeros((tq, DIFF_W), F32)]
        al = [jnp.zeros((tq, DIFF_W), F32), jnp.zeros((tq, DIFF_W), F32)]
        for h in range(DIFF_H):
            head_lanes = lane // DIFF_V == h
            trow = jnp.broadcast_to(tab_ref[h:h + 1, :], (tq, LANE))
            bias = jnp.concatenate(
                [jnp.take_along_axis(trow, n[:, c * LANE:(c + 1) * LANE], axis=1) for c in range(tk // LANE)], axis=1)
            vm = jnp.where(head_lanes, vb, jnp.zeros_like(vb))
            for w in range(2):
                c = 2 * h + w
                s = jnp.where(mask, _dot_t(qm_scr[c], kb) + bias, NEG)
                m_prev = m_scr[c]
                m_new = jnp.maximum(m_prev, jnp.max(s, axis=-1, keepdims=True))
                alpha = jnp.exp(m_prev - m_new)
                p = jnp.exp(s - m_new)
                l_scr[c] = alpha * l_scr[c] + jnp.sum(p, axis=-1, keepdims=True)
                m_scr[c] = m_new
                pv[w] = pv[w] + _dot(p.astype(BF16), vm)
                al[w] = jnp.where(head_lanes, alpha, al[w])
        acc1_scr[...] = acc1_scr[...] * al[0] + pv[0]
        acc2_scr[...] = acc2_scr[...] * al[1] + pv[1]

    @pl.when(j == nb - 1)
    def _fin():
        lq = lqk_ref[...]
        lam = (jnp.exp(jnp.sum(lq[0:1] * lq[1:2], axis=-1, keepdims=True))
               - jnp.exp(jnp.sum(lq[2:3] * lq[3:4], axis=-1, keepdims=True)) + lam_init)
        ls = [jnp.zeros((tq, DIFF_W), F32), jnp.zeros((tq, DIFF_W), F32)]
        for h in range(DIFF_H):
            for w in range(2):
                ls[w] = jnp.where(lane // DIFF_V == h, l_scr[2 * h + w], ls[w])
        out = acc1_scr[...] / ls[0] - lam * (acc2_scr[...] / ls[1])
        o_ref[...] = (_seg_norm(out, bd_ref[...], DIFF_V) * gsub_ref[...]).astype(o_ref.dtype)


def _diff_attn(q, k, v, positions, p, lam_init):
    b, s = positions.shape
    blk = _tile(s, 256)
    nb = s // blk
    bmin, bmax, last = _block_tables(positions, blk)
    q3, k3, v3 = (a.reshape(b, s, DIFF_W) for a in (q, k, v))
    kv_spec = pl.BlockSpec((None, blk, DIFF_W), lambda bb, i, j, mn, mx, la: (bb, jnp.minimum(j, la[bb * nb + i]), 0))
    params = [p["tab"], p["lqk"], p["gsub"], p["bd64"]]
    out = pl.pallas_call(
        functools.partial(_diff_attn_body, nb=nb, lam_init=lam_init),
        out_shape=jax.ShapeDtypeStruct((b, s, DIFF_W), BF16),
        grid_spec=pltpu.PrefetchScalarGridSpec(
            num_scalar_prefetch=3,
            grid=(b, nb, nb),
            in_specs=[pl.BlockSpec((None, blk, DIFF_W), lambda bb, i, j, *_: (bb, i, 0)), kv_spec, kv_spec,
                      pl.BlockSpec((None, blk, 1), lambda bb, i, j, *_: (bb, i, 0)),
                      pl.BlockSpec((None, 1, blk), lambda bb, i, j, mn, mx, la: (bb, 0, jnp.minimum(j, la[bb * nb + i])))]
            + [_full(a.shape) for a in params],
            out_specs=pl.BlockSpec((None, blk, DIFF_W), lambda bb, i, j, *_: (bb, i, 0)),
            scratch_shapes=[pltpu.VMEM((2 * DIFF_H, blk, DIFF_W), BF16),
                            pltpu.VMEM((2 * DIFF_H, blk, 1), F32), pltpu.VMEM((2 * DIFF_H, blk, 1), F32),
                            pltpu.VMEM((blk, DIFF_W), F32), pltpu.VMEM((blk, DIFF_W), F32)]),
        compiler_params=_cparams(("parallel", "parallel", "arbitrary")),
        name="diff_attn",
    )(bmin, bmax, last, q3, k3, v3, positions.reshape(b, s, 1), positions.reshape(b, 1, s), *params)
    return out.reshape(b * s, DIFF_W)


def _cross_kv_body(m_ref, g_ref, wk_ref, wv_ref, gk_ref, bd_ref, k_ref, v_ref):
    hm = _rms(m_ref[...], g_ref[...]).astype(BF16)
    k = _dot(hm, wk_ref[...])
    k_ref[...] = (_seg_norm(k, bd_ref[...], CROSS_D) * gk_ref[...]).astype(k_ref.dtype)
    v_ref[...] = _dot(hm, wv_ref[...]).astype(v_ref.dtype)


def _cross_kv(mem, p):
    b, n, d = mem.shape
    params = [p["gmem"], p["wk"], p["wv"], p["gk"], p["bd64"]]
    return pl.pallas_call(
        _cross_kv_body,
        out_shape=[jax.ShapeDtypeStruct((b, n, CROSS_W), BF16)] * 2,
        grid=(b,),
        in_specs=[pl.BlockSpec((None, n, d), lambda i: (i, 0, 0))] + [_full(a.shape) for a in params],
        out_specs=[pl.BlockSpec((None, n, CROSS_W), lambda i: (i, 0, 0))] * 2,
        compiler_params=_cparams(("parallel",)),
        name="cross_kv",
    )(mem, *params)


def _merge_body(x_ref, yrec_ref, ymla_ref, ydiff_ref, kx_ref, vx_ref, gmix_ref, wg_ref, bg_ref,
                ps5_ref, pmla_ref, plru_ref, pdiff_ref, wout_ref,
                gcross_ref, wq_ref, gq_ref, bd_ref, wo_ref, o_ref):
    x = x_ref[...]
    d = x.shape[1]
    h = _rms(x, gmix_ref[...]).astype(BF16)
    branches = ((yrec_ref[:, 0:S5_W], ps5_ref), (ymla_ref[...], pmla_ref),
                (yrec_ref[:, S5_W:S5_W + LRU_W], plru_ref), (ydiff_ref[...], pdiff_ref))
    merged = jnp.zeros(x.shape, F32)
    for n, (y, p_ref) in enumerate(branches):
        gate = jax.nn.sigmoid(_dot(h, wg_ref[:, n * d:(n + 1) * d]) + bg_ref[:, n * d:(n + 1) * d])
        merged = merged + gate * _dot(y, p_ref[...])
    x1 = x + _dot(merged.astype(BF16), wout_ref[...])

    hc = _rms(x1, gcross_ref[...]).astype(BF16)
    q = _dot(hc, wq_ref[...])
    q = (_seg_norm(q, bd_ref[...], CROSS_D) * gq_ref[...]).astype(BF16)
    lane = lax.broadcasted_iota(I32, (1, CROSS_W), 1)
    kx = kx_ref[...]
    vx = vx_ref[...]
    o = jnp.zeros((x.shape[0], CROSS_W), F32)
    for hd in range(CROSS_H):
        head_lanes = lane // CROSS_D == hd
        s = _dot_t(jnp.where(head_lanes, q, jnp.zeros_like(q)), kx)
        p = jnp.exp(s - jnp.max(s, axis=-1, keepdims=True))
        p = p / jnp.sum(p, axis=-1, keepdims=True)
        o = o + _dot(p.astype(BF16), jnp.where(head_lanes, vx, jnp.zeros_like(vx)))
    o_ref[...] = x1 + _dot(o.astype(BF16), wo_ref[...])


def _merge(x, yrec, ymla, ydiff, kx, vx, p, seq):
    t, d = x.shape
    tm = _tile(seq, 512)
    per_b = seq // tm
    nm = kx.shape[1]
    params = [p["gmix"], p["wg"], p["bg"], p["ps5"], p["pmla"], p["plru"], p["pdiff"], p["wout"],
              p["gcross"], p["wq"], p["gq"], p["bd64"], p["wo"]]
    row = lambda w: pl.BlockSpec((tm, w), lambda i: (i, 0))
    mem_spec = pl.BlockSpec((None, nm, CROSS_W), lambda i: (i // per_b, 0, 0))
    return pl.pallas_call(
        _merge_body,
        out_shape=jax.ShapeDtypeStruct((t, d), F32),
        grid=(t // tm,),
        in_specs=[row(d), row(yrec.shape[1]), row(ymla.shape[1]), row(ydiff.shape[1]), mem_spec, mem_spec]
        + [_full(a.shape) for a in params],
        out_specs=row(d),
        compiler_params=_cparams(("parallel",), 0.85),
        name="merge_cross",
    )(x, yrec, ymla, ydiff, kx, vx, *params)


def _ffn_body(x_ref, g_ref, wg_ref, wu_ref, wd_ref, o_ref, h_scr, acc_scr):
    f = pl.program_id(1)

    @pl.when(f == 0)
    def _init():
        h_scr[...] = _rms(x_ref[...], g_ref[...]).astype(BF16)
        acc_scr[...] = jnp.zeros(acc_scr.shape, F32)

    h = h_scr[...]
    a = jax.nn.silu(_dot(h, wg_ref[...])) * _dot(h, wu_ref[...])
    acc_scr[...] += _dot(a.astype(BF16), wd_ref[...])

    @pl.when(f == pl.num_programs(1) - 1)
    def _fin():
        o_ref[...] = x_ref[...] + acc_scr[...]


def _ffn(x, g, wg, wu, wd):
    t, d = x.shape
    ff = wg.shape[1]
    tm = _tile(t, 1024)
    tf = _tile(ff, 256)
    return pl.pallas_call(
        _ffn_body,
        out_shape=jax.ShapeDtypeStruct((t, d), F32),
        grid=(t // tm, ff // tf),
        in_specs=[pl.BlockSpec((tm, d), lambda i, f: (i, 0)), _full(g.shape),
                  pl.BlockSpec((d, tf), lambda i, f: (0, f)), pl.BlockSpec((d, tf), lambda i, f: (0, f)),
                  pl.BlockSpec((tf, d), lambda i, f: (f, 0))],
        out_specs=pl.BlockSpec((tm, d), lambda i, f: (i, 0)),
        scratch_shapes=[pltpu.VMEM((tm, d), BF16), pltpu.VMEM((tm, d), F32)],
        compiler_params=_cparams(("parallel", "arbitrary")),
        name="ffn",
    )(x, g, wg, wu, wd)


def _router_body(x_ref, g_ref, wr_ref, h_ref, r_ref):
    hf = _rms(x_ref[...], g_ref[...])
    h_ref[...] = hf.astype(h_ref.dtype)
    logits = jnp.dot(hf, wr_ref[...], preferred_element_type=F32, precision=HI)
    lane = lax.broadcasted_iota(I32, logits.shape, 1).astype(F32)
    logits = jnp.where(lane < N_EXPERTS, logits, -jnp.inf)
    m1 = jnp.max(logits, axis=-1, keepdims=True)
    i1 = jnp.min(jnp.where(logits == m1, lane, float(LANE)), axis=-1, keepdims=True)
    rest = jnp.where(lane == i1, -jnp.inf, logits)
    m2 = jnp.max(rest, axis=-1, keepdims=True)
    i2 = jnp.min(jnp.where(rest == m2, lane, float(LANE)), axis=-1, keepdims=True)
    e2 = jnp.exp(m2 - m1)
    w1 = 1.0 / (1.0 + e2)
    w2 = e2 / (1.0 + e2)
    r_ref[...] = jnp.where(lane == 0, i1, jnp.where(lane == 1, i2, jnp.where(lane == 2, w1, jnp.where(lane == 3, w2, 0.0))))


def _router(x, g, w_router_pad):
    t, d = x.shape
    tm = _tile(t, 512)
    return pl.pallas_call(
        _router_body,
        out_shape=[jax.ShapeDtypeStruct((t, d), F32), jax.ShapeDtypeStruct((t, LANE), F32)],
        grid=(t // tm,),
        in_specs=[pl.BlockSpec((tm, d), lambda i: (i, 0)), _full(g.shape), _full(w_router_pad.shape)],
        out_specs=[pl.BlockSpec((tm, d), lambda i: (i, 0)), pl.BlockSpec((tm, LANE), lambda i: (i, 0))],
        compiler_params=_cparams(("parallel",)),
        name="moe_router",
    )(x, g, w_router_pad)


def _gather_body(tok_ref, h_ref, xs_ref, sem, *, rows):
    base = pl.program_id(0) * rows

    def copy(r):
        return pltpu.make_async_copy(h_ref.at[pl.ds(tok_ref[base + r], 1)], xs_ref.at[pl.ds(base + r, 1)], sem)

    def start(r, c):
        copy(r).start()
        return c

    def wait(r, c):
        copy(r).wait()
        return c

    lax.fori_loop(0, rows, start, 0)
    lax.fori_loop(0, rows, wait, 0)


def _gather_rows(row_tok, h, n_rows):
    d = h.shape[1]
    rows = MOE_ROWS
    return pl.pallas_call(
        functools.partial(_gather_body, rows=rows),
        out_shape=jax.ShapeDtypeStruct((n_rows, d), h.dtype),
        grid_spec=pltpu.PrefetchScalarGridSpec(
            num_scalar_prefetch=1,
            grid=(n_rows // rows,),
            in_specs=[pl.BlockSpec(memory_space=pl.ANY)],
            out_specs=pl.BlockSpec(memory_space=pl.ANY),
            scratch_shapes=[pltpu.SemaphoreType.DMA]),
        compiler_params=_cparams(("arbitrary",)),
        name="moe_gather",
    )(row_tok, h)


def _expert_body(be_ref, bv_ref, xs_ref, rw_ref, wg_ref, wu_ref, wd_ref, o_ref, acc_scr):
    i, f = pl.program_id(0), pl.program_id(1)
    nf = pl.num_programs(1)

    @pl.when(f == 0)
    def _init():
        acc_scr[...] = jnp.zeros(acc_scr.shape, F32)

    @pl.when(bv_ref[i] > 0)
    def _compute():
        x = xs_ref[...].astype(BF16)
        a = jax.nn.silu(_dot(x, wg_ref[...])) * _dot(x, wu_ref[...])
        acc_scr[...] += _dot(a.astype(BF16), wd_ref[...])

    @pl.when(f == nf - 1)
    def _fin():
        o_ref[...] = acc_scr[...] * rw_ref[...]


def _experts(blk_e, blk_valid, xs, row_w, wg, wu, wd):
    n_rows, d = xs.shape
    ff = wg.shape[2]
    rows = MOE_ROWS
    tf = _tile(ff, 512)
    return pl.pallas_call(
        _expert_body,
        out_shape=jax.ShapeDtypeStruct((n_rows, d), F32),
        grid_spec=pltpu.PrefetchScalarGridSpec(
            num_scalar_prefetch=2,
            grid=(n_rows // rows, ff // tf),
            in_specs=[pl.BlockSpec((rows, d), lambda i, f, be, bv: (i, 0)),
                      pl.BlockSpec((rows, 1), lambda i, f, be, bv: (i, 0)),
                      pl.BlockSpec((None, d, tf), lambda i, f, be, bv: (be[i], 0, f)),
                      pl.BlockSpec((None, d, tf), lambda i, f, be, bv: (be[i], 0, f)),
                      pl.BlockSpec((None, tf, d), lambda i, f, be, bv: (be[i], f, 0))],
            out_specs=pl.BlockSpec((rows, d), lambda i, f, be, bv: (i, 0)),
            scratch_shapes=[pltpu.VMEM((rows, d), F32)]),
        compiler_params=_cparams(("parallel", "arbitrary")),
        name="moe_experts",
    )(blk_e, blk_valid, xs, row_w, wg, wu, wd)


def _combine_body(dest_ref, x_ref, ys_ref, o_ref, buf, sem, *, tm):
    base = pl.program_id(0) * tm

    def copy(r, k):
        return pltpu.make_async_copy(ys_ref.at[pl.ds(dest_ref[(base + r) * TOP_K + k], 1)],
                                     buf.at[k, pl.ds(r, 1)], sem)

    def start(r, c):
        for k in range(TOP_K):
            copy(r, k).start()
        return c

    def wait(r, c):
        for k in range(TOP_K):
            copy(r, k).wait()
        return c

    lax.fori_loop(0, tm, start, 0)
    lax.fori_loop(0, tm, wait, 0)
    o_ref[...] = x_ref[...] + sum(buf[k] for k in range(TOP_K))


def _combine(dest, x, ys):
    t, d = x.shape
    tm = _tile(t, 256)
    return pl.pallas_call(
        functools.partial(_combine_body, tm=tm),
        out_shape=jax.ShapeDtypeStruct((t, d), F32),
        grid_spec=pltpu.PrefetchScalarGridSpec(
            num_scalar_prefetch=1,
            grid=(t // tm,),
            in_specs=[pl.BlockSpec((tm, d), lambda i, de: (i, 0)), pl.BlockSpec(memory_space=pl.ANY)],
            out_specs=pl.BlockSpec((tm, d), lambda i, de: (i, 0)),
            scratch_shapes=[pltpu.VMEM((TOP_K, tm, d), F32), pltpu.SemaphoreType.DMA]),
        compiler_params=_cparams(("arbitrary",)),
        name="moe_combine",
    )(dest, x, ys)


def _moe(x, g, w_router, wg, wu, wd):
    t, d = x.shape
    n_assign = t * TOP_K
    rows = MOE_ROWS
    n_blk = -(-(n_assign + N_EXPERTS * (rows - 1)) // rows)
    n_rows = n_blk * rows
    wr_pad = jnp.zeros((d, LANE), F32).at[:, :N_EXPERTS].set(w_router)
    h, r = _router(x, g, wr_pad)
    top_e = r[:, 0:TOP_K].astype(I32)
    top_w = r[:, TOP_K:2 * TOP_K]
    flat_e = top_e.reshape(-1)
    onehot = (flat_e[:, None] == jnp.arange(N_EXPERTS, dtype=I32)[None, :]).astype(I32)
    csum = jnp.cumsum(onehot, axis=0)
    rank = jnp.sum(csum * onehot, axis=1) - 1
    counts = csum[-1]
    padded = (counts + rows - 1) // rows * rows
    pad_end = jnp.cumsum(padded)
    pad_start = pad_end - padded
    dest = (pad_start[flat_e] + rank).astype(I32)
    flat_t = jnp.repeat(jnp.arange(t, dtype=I32), TOP_K)
    row_tok = jnp.zeros((n_rows,), I32).at[dest].set(flat_t)
    row_w = jnp.zeros((n_rows,), F32).at[dest].set(top_w.reshape(-1))
    blk_start = jnp.arange(n_blk, dtype=I32) * rows
    blk_e = jnp.minimum(jnp.searchsorted(pad_end, blk_start, side="right"), N_EXPERTS - 1).astype(I32)
    blk_valid = (blk_start < pad_end[-1]).astype(I32)
    xs = _gather_rows(row_tok, h, n_rows)
    ys = _experts(blk_e, blk_valid, xs, row_w.reshape(n_rows, 1), wg, wu, wd)
    return _combine(dest, x, ys)


def _slot_pad(w, n_heads, width):
    lead = w.shape[:-1]
    w = w.reshape(lead + (n_heads, width))
    w = jnp.pad(w, [(0, 0)] * len(lead) + [(0, 0), (0, LANE - width)])
    return w.reshape(lead + (n_heads * LANE,))


def _rope_swap(w):
    half = MLA_ROPE // 2
    z = jnp.zeros(w.shape[:-1] + (MLA_NOPE,), w.dtype)
    tail = jnp.zeros(w.shape[:-1] + (LANE - MLA_QK,), w.dtype)
    return jnp.concatenate([z, w[..., MLA_NOPE + half:MLA_QK], w[..., MLA_NOPE:MLA_NOPE + half], tail], axis=-1)


def _block_diag(w):
    n, a, b = w.shape
    return jnp.einsum("nab,nm->namb", w, jnp.eye(n, dtype=w.dtype)).reshape(n * a, n * b)


def _rel_bucket_by_distance():
    n = np.arange(REL_MAX_DIST)
    exact = REL_BUCKETS // 2
    log_ratio = np.log(np.maximum(n, exact).astype(np.float32) / exact) / math.log(REL_MAX_DIST / exact)
    large = np.minimum(exact + (log_ratio * (REL_BUCKETS - exact)).astype(np.int32), REL_BUCKETS - 1)
    return np.where(n < exact, n, large).astype(np.int32)


def kernel(x, mem, positions, rel_table, g_mix, g_cross, g_mem, g_ffn, w_in, b_gate, s5_lam_re, s5_lam_im, s5_log_step, s5_b_re, s5_b_im, s5_c_re, s5_c_im, s5_d, s5_w_glu, s5_b_glu, mla_g_cq, mla_g_ckv, mla_w_uq, mla_w_ukv, mla_g_qn, mla_g_kn, lru_conv_w, lru_conv_b, lru_w_r, lru_b_r, lru_w_i, lru_b_i, lru_lam, diff_g_qn, diff_g_kn, diff_lq1, diff_lk1, diff_lq2, diff_lk2, diff_g_sub, w_branch, w_out, x_wq, x_wk, x_wv, x_wo, x_g_qn, x_g_kn, ffn_w_gate, ffn_w_up, ffn_w_down, moe_w_router, moe_w_gate, moe_w_up, moe_w_down):
    b, s, d = x.shape
    t = b * s
    depth = w_in.shape[0]
    positions = positions.astype(I32)
    pos_col = positions.reshape(t, 1)
    slots = -(-b // SUB) * SUB
    row2 = lambda a: a.reshape(1, -1).astype(F32)
    bd32 = jnp.asarray(np.kron(np.eye(DIFF_W // DIFF_D), np.ones((DIFF_D, DIFF_D))), F32)
    bd64 = jnp.asarray(np.kron(np.eye(CROSS_H), np.ones((CROSS_D, CROSS_D))), F32)
    half = MLA_ROPE // 2
    inv = ROPE_THETA ** (-np.arange(half, dtype=np.float32) / half)
    inv_full = np.zeros((1, LANE), np.float32)
    inv_full[0, MLA_NOPE:MLA_NOPE + half] = inv
    inv_full[0, MLA_NOPE + half:MLA_QK] = inv
    sgn = np.zeros((1, LANE), np.float32)
    sgn[0, MLA_NOPE:MLA_NOPE + half] = -1.0
    sgn[0, MLA_NOPE + half:MLA_QK] = 1.0
    bucket = _rel_bucket_by_distance()

    xf = x.reshape(t, d)
    for l in range(depth):
        lam_init = 0.8 - 0.6 * math.exp(-0.3 * l)
        w = w_in[l]
        off = np.cumsum([0, S5_W, MLA_QR, MLA_KVR, MLA_ROPE, LRU_W, LRU_W, DIFF_W, DIFF_W, DIFF_W])
        c_s5, c_cq, c_ckv, c_kpe, c_lx, c_lg, c_qd, c_kd, c_vd = (w[:, off[i]:off[i + 1]] for i in range(9))
        c_gate = w[:, off[9]:]
        zpad = lambda n: jnp.zeros((d, n), F32)
        kpe_pos = jnp.concatenate([zpad(MLA_NOPE), c_kpe, zpad(LANE - MLA_QK)], axis=1)
        kpe_sw = jnp.concatenate([zpad(MLA_NOPE), c_kpe[:, half:], c_kpe[:, :half], zpad(LANE - MLA_QK)], axis=1)
        w_rec = jnp.concatenate([c_s5, c_lx, c_lg], axis=1).astype(BF16)
        w_mla = jnp.concatenate([c_cq, zpad(2 * LANE - MLA_QR), c_ckv, kpe_pos, kpe_sw], axis=1).astype(BF16)
        w_diff = jnp.concatenate([c_qd, c_kd, c_vd], axis=1).astype(BF16)
        z_rec, z_mla, z_diff = _inproj(xf, row2(g_mix[l]), w_rec, w_mla, w_diff)

        dt = jnp.exp(s5_log_step[l])[:, None]
        lr, li = s5_lam_re[l], s5_lam_im[l]
        mag = jnp.exp(lr * dt)
        ar, ai = mag * jnp.cos(li * dt), mag * jnp.sin(li * dt)
        den = lr * lr + li * li
        zr = ((ar - 1.0) * lr + ai * li) / den
        zi = (ai * lr - (ar - 1.0) * li) / den
        bbr = zr[..., None] * s5_b_re[l] - zi[..., None] * s5_b_im[l]
        bbi = zr[..., None] * s5_b_im[l] + zi[..., None] * s5_b_re[l]
        rec_p = dict(
            ar=row2(ar), ai=row2(ai),
            bre=_block_diag(jnp.swapaxes(bbr, 1, 2)).astype(BF16), bim=_block_diag(jnp.swapaxes(bbi, 1, 2)).astype(BF16),
            cre=_block_diag(jnp.swapaxes(s5_c_re[l], 1, 2)).astype(BF16),
            cim=_block_diag(jnp.swapaxes(-s5_c_im[l], 1, 2)).astype(BF16),
            d=row2(s5_d[l]), wglu=s5_w_glu[l].astype(BF16), bglu=row2(s5_b_glu[l]),
            cw=lru_conv_w[l].astype(F32), cb=row2(lru_conv_b[l]),
            wr=_block_diag(lru_w_r[l]).astype(BF16), br=row2(lru_b_r[l]),
            wi=_block_diag(lru_w_i[l]).astype(BF16), bi=row2(lru_b_i[l]), lam=row2(lru_lam[l]))
        zt = jnp.swapaxes(z_rec.reshape(b, s, -1), 0, 1)
        zt = jnp.pad(zt, ((0, 0), (0, slots - b), (0, 0))).reshape(s * slots, -1)
        y_rec = _recurrent(zt, rec_p).reshape(s, slots, -1)[:, :b]
        y_rec = jnp.swapaxes(y_rec, 0, 1).reshape(t, -1)

        wuq = mla_w_uq[l].reshape(MLA_QR, MLA_H, MLA_QK)
        wukv = mla_w_ukv[l].reshape(MLA_KVR, MLA_H, MLA_NOPE + MLA_V)
        rows_pad = lambda a: jnp.pad(a, ((0, 2 * LANE - MLA_QR), (0, 0)))
        mla_p = dict(
            gcq=jnp.pad(row2(mla_g_cq[l]), ((0, 0), (0, 2 * LANE - MLA_QR))), gckv=row2(mla_g_ckv[l]),
            wq=rows_pad(_slot_pad(mla_w_uq[l], MLA_H, MLA_QK)).astype(BF16),
            wqs=rows_pad(_rope_swap(wuq).reshape(MLA_QR, MLA_H * LANE)).astype(BF16),
            wk=_slot_pad(wukv[..., :MLA_NOPE].reshape(MLA_KVR, -1), MLA_H, MLA_NOPE).astype(BF16),
            wv=_slot_pad(wukv[..., MLA_NOPE:].reshape(MLA_KVR, -1), MLA_H, MLA_V).astype(BF16),
            gq=_slot_pad(row2(mla_g_qn[l]), 1, MLA_QK), gqs=_rope_swap(row2(mla_g_qn[l])),
            gk=_slot_pad(row2(mla_g_kn[l]), 1, MLA_QK), gks=_rope_swap(row2(mla_g_kn[l])),
            inv=jnp.asarray(inv_full), sgn=jnp.asarray(sgn))
        q_m, k_m, v_m = _mla_prep(z_mla, pos_col, mla_p)
        y_mla = _mla_attn(q_m, k_m, v_m, positions)

        diff_p = dict(
            gq=jnp.tile(row2(diff_g_qn[l]), (1, DIFF_W // DIFF_D)) * DIFF_D ** -0.5,
            gk=jnp.tile(row2(diff_g_kn[l]), (1, DIFF_W // DIFF_D)), bd32=bd32, bd64=bd64,
            tab=jnp.transpose(rel_table[bucket]).astype(F32),
            lqk=jnp.pad(jnp.stack([diff_lq1[l], diff_lk1[l], diff_lq2[l], diff_lk2[l]]).astype(F32),
                        ((0, SUB - 4), (0, LANE - DIFF_D))),
            gsub=jnp.tile(row2(diff_g_sub[l]), (1, DIFF_H)) * (1.0 - lam_init))
        q_d, k_d, v_d = _diff_prep(z_diff, diff_p)
        y_diff = _diff_attn(q_d, k_d, v_d, positions, diff_p, lam_init)

        cross_p = dict(gmem=row2(g_mem[l]), wk=x_wk[l].astype(BF16), wv=x_wv[l].astype(BF16),
                       gk=jnp.tile(row2(x_g_kn[l]), (1, CROSS_H)), bd64=bd64)
        kx, vx = _cross_kv(mem, cross_p)
        pm = w_branch[l, 1].reshape(MLA_H, MLA_V, d)
        merge_p = dict(
            gmix=row2(g_mix[l]), wg=c_gate.astype(BF16), bg=row2(b_gate[l]),
            ps5=w_branch[l, 0].astype(BF16),
            pmla=jnp.pad(pm, ((0, 0), (0, LANE - MLA_V), (0, 0))).reshape(MLA_H * LANE, d).astype(BF16),
            plru=w_branch[l, 2].astype(BF16), pdiff=w_branch[l, 3].astype(BF16), wout=w_out[l].astype(BF16),
            gcross=row2(g_cross[l]), wq=x_wq[l].astype(BF16),
            gq=jnp.tile(row2(x_g_qn[l]), (1, CROSS_H)) * CROSS_D ** -0.5, bd64=bd64, wo=x_wo[l].astype(BF16))
        xf = _merge(xf, y_rec, y_mla, y_diff, kx, vx, merge_p, s)

        if l % 2 == 0:
            e = l // 2
            xf = _ffn(xf, row2(g_ffn[l]), ffn_w_gate[e].astype(BF16), ffn_w_up[e].astype(BF16),
                      ffn_w_down[e].astype(BF16))
        else:
            e = l // 2
            xf = _moe(xf, row2(g_ffn[l]), moe_w_router[e], moe_w_gate[e].astype(BF16),
                      moe_w_up[e].astype(BF16), moe_w_down[e].astype(BF16))
    return xf.reshape(b, s, d)
```

```python
import functools
import math

import numpy as np
import jax
import jax.numpy as jnp
from jax import lax
from jax.experimental import pallas as pl
from jax.experimental.pallas import tpu as pltpu

F32 = jnp.float32
BF16 = jnp.bfloat16
I32 = jnp.int32
EPS = 1e-6
NEG = -1e30
HI = lax.Precision.HIGHEST

LANE = 128
SUB = 8
VMEM_BYTES = 64 * 1024 * 1024

S5_W, S5_GROUP, S5_STATE = 256, 16, 64
S5_GROUPS = S5_W // S5_GROUP
S5_N = S5_GROUPS * S5_STATE
MLA_H, MLA_QR, MLA_KVR, MLA_NOPE, MLA_ROPE, MLA_V = 4, 192, 128, 64, 32, 64
MLA_QK = MLA_NOPE + MLA_ROPE
ROPE_THETA = 10000.0
LRU_W, LRU_BLOCKS, LRU_CONV, LRU_C = 256, 4, 4, 8.0
DIFF_H, DIFF_D = 4, 32
DIFF_V = 2 * DIFF_D
DIFF_W = DIFF_H * DIFF_V
REL_BUCKETS, REL_MAX_DIST = 32, 128
CROSS_H, CROSS_D = 4, 64
CROSS_W = CROSS_H * CROSS_D
N_EXPERTS, TOP_K = 8, 2
N_BRANCH, BRANCH_W = 4, 256
MOE_ROWS = 512


def _cparams(sem, frac=0.75, flags=None):
    return pltpu.CompilerParams(dimension_semantics=sem, vmem_limit_bytes=int(VMEM_BYTES * frac), flags=flags)


def _tile(n, pref):
    t = min(n, pref)
    while n % t:
        t -= SUB
    return t


def _rms(x, g):
    return x * lax.rsqrt(jnp.mean(x * x, axis=-1, keepdims=True) + EPS) * g


def _full(shape):
    nd = len(shape)
    return pl.BlockSpec(shape, lambda *_: (0,) * nd)


def _dot(a, b):
    return jnp.dot(a, b, preferred_element_type=F32)


def _dot_t(a, b):
    return lax.dot_general(a, b, (((1,), (1,)), ((), ())), preferred_element_type=F32)


def _seg_norm(a, bd, width):
    ss = jnp.dot(a * a, bd, preferred_element_type=F32, precision=HI)
    return a * lax.rsqrt(ss * (1.0 / width) + EPS)


def _inproj_body(x_ref, g_ref, wr_ref, wm_ref, wd_ref, zr_ref, zm_ref, zd_ref):
    h = _rms(x_ref[...], g_ref[...]).astype(BF16)
    zr_ref[...] = _dot(h, wr_ref[...])
    zm_ref[...] = _dot(h, wm_ref[...])
    zd_ref[...] = _dot(h, wd_ref[...])


def _inproj(x, g, w_rec, w_mla, w_diff):
    t, d = x.shape
    tm = _tile(t, 512)
    outs = [jax.ShapeDtypeStruct((t, w.shape[1]), F32) for w in (w_rec, w_mla, w_diff)]
    return pl.pallas_call(
        _inproj_body,
        out_shape=outs,
        grid=(t // tm,),
        in_specs=[pl.BlockSpec((tm, d), lambda i: (i, 0)), _full(g.shape),
                  _full(w_rec.shape), _full(w_mla.shape), _full(w_diff.shape)],
        out_specs=[pl.BlockSpec((tm, w.shape[1]), lambda i: (i, 0)) for w in (w_rec, w_mla, w_diff)],
        compiler_params=_cparams(("parallel",)),
        name="inproj",
    )(x, g, w_rec, w_mla, w_diff)


def _recurrent_body(z_ref, ar_ref, ai_ref, bre_ref, bim_ref, cre_ref, cim_ref, d_ref, wglu_ref, bglu_ref,
                    cw_ref, cb_ref, wr_ref, br_ref, wi_ref, bi_ref, lam_ref,
                    o_ref, bur, bui, st_re, st_im, xbuf, a_scr, h_scr, st_lru, *, steps):
    rows = steps * SUB
    halo = (LRU_CONV - 1) * SUB

    @pl.when(pl.program_id(0) == 0)
    def _init():
        st_re[...] = jnp.zeros(st_re.shape, F32)
        st_im[...] = jnp.zeros(st_im.shape, F32)
        st_lru[...] = jnp.zeros(st_lru.shape, F32)
        xbuf[0:halo, :] = jnp.zeros((halo, LRU_W), F32)

    u = z_ref[:, 0:S5_W]
    ub = u.astype(BF16)
    bur[...] = _dot(ub, bre_ref[...])
    bui[...] = _dot(ub, bim_ref[...])
    ar = jnp.broadcast_to(ar_ref[...], (SUB, S5_N))
    ai = jnp.broadcast_to(ai_ref[...], (SUB, S5_N))

    def s5_step(t, carry):
        hr, hi = carry
        r = pl.multiple_of(t * SUB, SUB)
        nhr = ar * hr - ai * hi + bur[pl.ds(r, SUB), :]
        nhi = ar * hi + ai * hr + bui[pl.ds(r, SUB), :]
        bur[pl.ds(r, SUB), :] = nhr
        bui[pl.ds(r, SUB), :] = nhi
        return nhr, nhi

    hr, hi = lax.fori_loop(0, steps, s5_step, (st_re[...], st_im[...]))
    st_re[...] = hr
    st_im[...] = hi
    y = _dot(bur[...].astype(BF16), cre_ref[...]) + _dot(bui[...].astype(BF16), cim_ref[...]) + d_ref[...] * u
    y = jax.nn.gelu(y)
    y = y * jax.nn.sigmoid(_dot(y.astype(BF16), wglu_ref[...]) + bglu_ref[...])
    o_ref[:, 0:S5_W] = y.astype(o_ref.dtype)

    xbuf[halo:halo + rows, :] = z_ref[:, S5_W:S5_W + LRU_W]
    xc = cb_ref[...] + sum(xbuf[k * SUB:k * SUB + rows, :] * cw_ref[k:k + 1, :] for k in range(LRU_CONV))
    xbuf[0:halo, :] = xbuf[rows:rows + halo, :]
    xcb = xc.astype(BF16)
    rg = jax.nn.sigmoid(_dot(xcb, wr_ref[...]) + br_ref[...])
    ig = jax.nn.sigmoid(_dot(xcb, wi_ref[...]) + bi_ref[...])
    nl = -lam_ref[...]
    softplus = jnp.maximum(nl, 0.0) + jnp.log1p(jnp.exp(-jnp.abs(nl)))
    log_a = -LRU_C * rg * softplus
    a_scr[...] = jnp.exp(log_a)
    th = jnp.tanh(log_a)
    h_scr[...] = jnp.sqrt(-2.0 * th / (1.0 - th)) * (ig * xc)

    def lru_step(t, h):
        r = pl.multiple_of(t * SUB, SUB)
        nh = a_scr[pl.ds(r, SUB), :] * h + h_scr[pl.ds(r, SUB), :]
        h_scr[pl.ds(r, SUB), :] = nh
        return nh

    st_lru[...] = lax.fori_loop(0, steps, lru_step, st_lru[...])
    gate = z_ref[:, S5_W + LRU_W:S5_W + 2 * LRU_W]
    o_ref[:, S5_W:S5_W + LRU_W] = (h_scr[...] * jax.nn.gelu(gate)).astype(o_ref.dtype)


def _recurrent(z, p):
    rows_total = z.shape[0]
    steps = _tile(rows_total // SUB, 64)
    rows = steps * SUB
    halo = (LRU_CONV - 1) * SUB
    params = [p["ar"], p["ai"], p["bre"], p["bim"], p["cre"], p["cim"], p["d"], p["wglu"], p["bglu"],
              p["cw"], p["cb"], p["wr"], p["br"], p["wi"], p["bi"], p["lam"]]
    return pl.pallas_call(
        functools.partial(_recurrent_body, steps=steps),
        out_shape=jax.ShapeDtypeStruct((rows_total, S5_W + LRU_W), BF16),
        grid=(rows_total // rows,),
        in_specs=[pl.BlockSpec((rows, z.shape[1]), lambda i: (i, 0))] + [_full(a.shape) for a in params],
        out_specs=pl.BlockSpec((rows, S5_W + LRU_W), lambda i: (i, 0)),
        scratch_shapes=[pltpu.VMEM((rows, S5_N), F32), pltpu.VMEM((rows, S5_N), F32),
                        pltpu.VMEM((SUB, S5_N), F32), pltpu.VMEM((SUB, S5_N), F32),
                        pltpu.VMEM((rows + halo, LRU_W), F32), pltpu.VMEM((rows, LRU_W), F32),
                        pltpu.VMEM((rows, LRU_W), F32), pltpu.VMEM((SUB, LRU_W), F32)],
        compiler_params=_cparams(("arbitrary",)),
        name="recurrent",
    )(z, *params)


def _mla_prep_body(z_ref, pos_ref, gcq_ref, gckv_ref, wq_ref, wqs_ref, wk_ref, wv_ref,
                   gq_ref, gqs_ref, gk_ref, gks_ref, inv_ref, sgn_ref, qt_ref, k_ref, vt_ref):
    cq = z_ref[:, 0:2 * LANE]
    ckv = z_ref[:, 2 * LANE:3 * LANE]
    kpe = z_ref[:, 3 * LANE:4 * LANE]
    kpe_sw = z_ref[:, 4 * LANE:5 * LANE]
    hq = (cq * lax.rsqrt(jnp.sum(cq * cq, axis=-1, keepdims=True) * (1.0 / MLA_QR) + EPS) * gcq_ref[...]).astype(BF16)
    hkv = _rms(ckv, gckv_ref[...]).astype(BF16)
    q = _dot(hq, wq_ref[...])
    qs = _dot(hq, wqs_ref[...])
    kn = _dot(hkv, wk_ref[...])
    vt_ref[...] = _dot(hkv, wv_ref[...]).T.astype(vt_ref.dtype)
    ang = pos_ref[...].astype(F32) * inv_ref[...]
    cos = jnp.cos(ang)
    sin = jnp.sin(ang) * sgn_ref[...]
    qc, qsn = gq_ref[...] * cos * (MLA_QK ** -0.5), gqs_ref[...] * sin * (MLA_QK ** -0.5)
    kc, ksn = gk_ref[...] * cos, gks_ref[...] * sin
    for h in range(MLA_H):
        sl = slice(h * LANE, (h + 1) * LANE)
        qh = q[:, sl]
        r = lax.rsqrt(jnp.sum(qh * qh, axis=-1, keepdims=True) * (1.0 / MLA_QK) + EPS)
        qt_ref[sl, :] = ((qh * qc + qs[:, sl] * qsn) * r).T.astype(qt_ref.dtype)
        kh = kn[:, sl] + kpe
        r = lax.rsqrt(jnp.sum(kh * kh, axis=-1, keepdims=True) * (1.0 / MLA_QK) + EPS)
        k_ref[:, sl] = ((kh * kc + kpe_sw * ksn) * r).astype(k_ref.dtype)


def _mla_prep(z, pos_col, p):
    t = z.shape[0]
    tm = _tile(t, 512)
    params = [p["gcq"], p["gckv"], p["wq"], p["wqs"], p["wk"], p["wv"],
              p["gq"], p["gqs"], p["gk"], p["gks"], p["inv"], p["sgn"]]
    width = MLA_H * LANE
    vw = MLA_H * MLA_V
    return pl.pallas_call(
        _mla_prep_body,
        out_shape=[jax.ShapeDtypeStruct((width, t), BF16), jax.ShapeDtypeStruct((t, width), BF16),
                   jax.ShapeDtypeStruct((vw, t), BF16)],
        grid=(t // tm,),
        in_specs=[pl.BlockSpec((tm, z.shape[1]), lambda i: (i, 0)), pl.BlockSpec((tm, 1), lambda i: (i, 0))]
        + [_full(a.shape) for a in params],
        out_specs=[pl.BlockSpec((width, tm), lambda i: (0, i)), pl.BlockSpec((tm, width), lambda i: (i, 0)),
                   pl.BlockSpec((vw, tm), lambda i: (0, i))],
        compiler_params=_cparams(("parallel",)),
        name="mla_prep",
    )(z, pos_col, *params)


def _block_tables(positions, blk):
    b, s = positions.shape
    nb = s // blk
    pb = positions.reshape(b, nb, blk)
    bmin = pb.min(-1)
    bmax = pb.max(-1)
    live = bmax[:, :, None] >= bmin[:, None, :]
    last = jnp.max(jnp.where(live, jnp.arange(nb, dtype=I32)[None, None, :], 0), axis=-1)
    return bmin.reshape(-1).astype(I32), bmax.reshape(-1).astype(I32), last.reshape(-1).astype(I32)


def _mla_attn_body(bmin_ref, bmax_ref, last_ref, qt_ref, k_ref, vt_ref, pq_ref, pk_ref, o_ref,
                   m_scr, l_scr, acc_scr, *, nb):
    b, i, j = pl.program_id(0), pl.program_id(1), pl.program_id(2)
    qi, kj = b * nb + i, b * nb + j

    @pl.when(j == 0)
    def _init():
        m_scr[...] = jnp.full(m_scr.shape, -jnp.inf, F32)
        l_scr[...] = jnp.zeros(l_scr.shape, F32)
        acc_scr[...] = jnp.zeros(acc_scr.shape, F32)

    def block(masked):
        if masked:
            mask = pq_ref[...] >= pk_ref[...]
        for h in range(MLA_H):
            s = _dot(k_ref[:, h * LANE:(h + 1) * LANE], qt_ref[h * LANE:(h + 1) * LANE, :])
            if masked:
                s = jnp.where(mask, s, NEG)
            rows = slice(h * MLA_V, (h + 1) * MLA_V)
            m_prev = m_scr[h:h + 1, :]
            m_new = jnp.maximum(m_prev, jnp.max(s, axis=0, keepdims=True))
            alpha = jnp.exp(m_prev - m_new)
            p = jnp.exp(s - m_new)
            l_scr[h:h + 1, :] = alpha * l_scr[h:h + 1, :] + jnp.sum(p, axis=0, keepdims=True)
            acc_scr[rows, :] = alpha * acc_scr[rows, :] + _dot(vt_ref[rows, :], p.astype(BF16))
            m_scr[h:h + 1, :] = m_new

    live = bmax_ref[qi] >= bmin_ref[kj]
    unmasked = bmin_ref[qi] >= bmax_ref[kj]

    @pl.when(jnp.logical_and(live, unmasked))
    def _past():
        block(False)

    @pl.when(jnp.logical_and(live, jnp.logical_not(unmasked)))
    def _diag():
        block(True)

    @pl.when(j == nb - 1)
    def _fin():
        for h in range(MLA_H):
            rows = slice(h * MLA_V, (h + 1) * MLA_V)
            acc_scr[rows, :] = acc_scr[rows, :] / l_scr[h:h + 1, :]
        o_ref[...] = acc_scr[...].T.astype(o_ref.dtype)


def _mla_attn(qt, k, vt, positions):
    b, s = positions.shape
    width = k.shape[-1]
    vw = vt.shape[0]
    blk = _tile(s, 512)
    nb = s // blk
    bmin, bmax, last = _block_tables(positions, blk)
    kcol = lambda bb, i, j, mn, mx, la: bb * nb + jnp.minimum(j, la[bb * nb + i])
    return pl.pallas_call(
        functools.partial(_mla_attn_body, nb=nb),
        out_shape=jax.ShapeDtypeStruct((b * s, vw), BF16),
        grid_spec=pltpu.PrefetchScalarGridSpec(
            num_scalar_prefetch=3,
            grid=(b, nb, nb),
            in_specs=[pl.BlockSpec((width, blk), lambda bb, i, j, *_: (0, bb * nb + i)),
                      pl.BlockSpec((blk, width), lambda *a: (kcol(*a), 0)),
                      pl.BlockSpec((vw, blk), lambda *a: (0, kcol(*a))),
                      pl.BlockSpec((1, blk), lambda bb, i, j, *_: (0, bb * nb + i)),
                      pl.BlockSpec((blk, 1), lambda *a: (kcol(*a), 0))],
            out_specs=pl.BlockSpec((blk, vw), lambda bb, i, j, *_: (bb * nb + i, 0)),
            scratch_shapes=[pltpu.VMEM((SUB, blk), F32), pltpu.VMEM((SUB, blk), F32),
                            pltpu.VMEM((vw, blk), F32)]),
        compiler_params=_cparams(("parallel", "parallel", "arbitrary")),
        name="mla_attn",
    )(bmin, bmax, last, qt, k, vt, positions.reshape(1, b * s), positions.reshape(b * s, 1))


def _diff_prep_body(z_ref, gq_ref, gk_ref, bd_ref, qt_ref, k_ref, vt_ref):
    bd = bd_ref[...]
    qt_ref[...] = (_seg_norm(z_ref[:, 0:DIFF_W], bd, DIFF_D) * gq_ref[...]).T.astype(qt_ref.dtype)
    k_ref[...] = (_seg_norm(z_ref[:, DIFF_W:2 * DIFF_W], bd, DIFF_D) * gk_ref[...]).astype(k_ref.dtype)
    vt_ref[...] = z_ref[:, 2 * DIFF_W:3 * DIFF_W].T.astype(vt_ref.dtype)


def _diff_prep(z, p):
    t = z.shape[0]
    tm = _tile(t, 512)
    params = [p["gq"], p["gk"], p["bd32"]]
    row = pl.BlockSpec((tm, DIFF_W), lambda i: (i, 0))
    col = pl.BlockSpec((DIFF_W, tm), lambda i: (0, i))
    return pl.pallas_call(
        _diff_prep_body,
        out_shape=[jax.ShapeDtypeStruct((DIFF_W, t), BF16), jax.ShapeDtypeStruct((t, DIFF_W), BF16),
                   jax.ShapeDtypeStruct((DIFF_W, t), BF16)],
        grid=(t // tm,),
        in_specs=[pl.BlockSpec((tm, z.shape[1]), lambda i: (i, 0))] + [_full(a.shape) for a in params],
        out_specs=[col, row, col],
        compiler_params=_cparams(("parallel",)),
        name="diff_prep",
    )(z, *params)


def _diff_attn_body(bmin_ref, bmax_ref, last_ref, qt_ref, k_ref, vt_ref, pq_ref, pk_ref, tab_ref, lqk_ref,
                    gsub_ref, o_ref, qm_scr, s_scr, p_scr, m_scr, l_scr, acc_scr, *, nb, lam_init):
    b, i, j = pl.program_id(0), pl.program_id(1), pl.program_id(2)
    qi, kj = b * nb + i, b * nb + j
    tq = qt_ref.shape[1]
    tk = k_ref.shape[0]
    n_streams = 2 * DIFF_H

    @pl.when(j == 0)
    def _init():
        m_scr[...] = jnp.full(m_scr.shape, -jnp.inf, F32)
        l_scr[...] = jnp.zeros(l_scr.shape, F32)
        acc_scr[...] = jnp.zeros(acc_scr.shape, F32)
        qv = qt_ref[...]
        feat = lax.broadcasted_iota(I32, (DIFF_W, 1), 0)
        for c in range(n_streams):
            qm_scr[:, c * tq:(c + 1) * tq] = jnp.where(feat // DIFF_D == c, qv, jnp.zeros_like(qv))

    def block(masked, far):
        s_scr[...] = _dot(k_ref[...], qm_scr[...])
        if not far:
            dist = pq_ref[...] - pk_ref[...]
            n = jnp.clip(dist, 0, REL_MAX_DIST - 1)
        for h in range(DIFF_H):
            cols = slice(2 * h * tq, (2 * h + 2) * tq)
            if far:
                bias = tab_ref[h:h + 1, REL_MAX_DIST - 1:REL_MAX_DIST]
            else:
                trow = jnp.broadcast_to(tab_ref[h:h + 1, :], (tk, LANE))
                bias = jnp.concatenate(
                    [jnp.take_along_axis(trow, n[:, c * LANE:(c + 1) * LANE], axis=1) for c in range(tq // LANE)],
                    axis=1)
                bias = jnp.concatenate([bias, bias], axis=1)
            sh = s_scr[:, cols] + bias
            if masked:
                mask = dist >= 0
                sh = jnp.where(jnp.concatenate([mask, mask], axis=1), sh, NEG)
            s_scr[:, cols] = sh
        s = s_scr[...]
        m_prev = m_scr[...]
        m_new = jnp.maximum(m_prev, jnp.max(s, axis=0, keepdims=True))
        alpha = jnp.exp(m_prev - m_new)
        p = jnp.exp(s - m_new)
        l_scr[...] = alpha * l_scr[...] + jnp.sum(p, axis=0, keepdims=True)
        m_scr[...] = m_new
        p_scr[...] = p.astype(BF16)
        for h in range(DIFF_H):
            rows = slice(h * DIFF_V, (h + 1) * DIFF_V)
            cols = slice(2 * h * tq, (2 * h + 2) * tq)
            acc_scr[rows, :] = alpha[:, cols] * acc_scr[rows, :] + _dot(vt_ref[rows, :], p_scr[:, cols])

    live = bmax_ref[qi] >= bmin_ref[kj]
    gap = bmin_ref[qi] - bmax_ref[kj]

    @pl.when(jnp.logical_and(live, gap < 0))
    def _diag():
        block(True, False)

    @pl.when(jnp.logical_and(gap >= 0, gap < REL_MAX_DIST - 1))
    def _near():
        block(False, False)

    @pl.when(gap >= REL_MAX_DIST - 1)
    def _far():
        block(False, True)

    @pl.when(j == nb - 1)
    def _fin():
        lq = lqk_ref[...]
        lam = (jnp.exp(jnp.sum(lq[0:1] * lq[1:2], axis=-1, keepdims=True))
               - jnp.exp(jnp.sum(lq[2:3] * lq[3:4], axis=-1, keepdims=True)) + lam_init)
        for h in range(DIFF_H):
            rows = slice(h * DIFF_V, (h + 1) * DIFF_V)
            c1 = slice(2 * h * tq, (2 * h + 1) * tq)
            c2 = slice((2 * h + 1) * tq, (2 * h + 2) * tq)
            out = (acc_scr[rows, 0:tq] / l_scr[:, c1] - lam * (acc_scr[rows, tq:2 * tq] / l_scr[:, c2]))
            ms = jnp.sum(out * out, axis=0, keepdims=True) * (1.0 / DIFF_V)
            acc_scr[rows, 0:tq] = out * lax.rsqrt(ms + EPS) * gsub_ref[rows, :]
        o_ref[...] = acc_scr[:, 0:tq].T.astype(o_ref.dtype)


def _diff_attn(qt, k, vt, positions, p, lam_init):
    b, s = positions.shape
    blk = _tile(s, 256)
    nb = s // blk
    wide = 2 * DIFF_H * blk
    bmin, bmax, last = _block_tables(positions, blk)
    kcol = lambda bb, i, j, mn, mx, la: bb * nb + jnp.minimum(j, la[bb * nb + i])
    gsub = jnp.broadcast_to(p["gsub"].reshape(DIFF_W, 1), (DIFF_W, blk))
    params = [p["tab"], p["lqk"], gsub]
    return pl.pallas_call(
        functools.partial(_diff_attn_body, nb=nb, lam_init=lam_init),
        out_shape=jax.ShapeDtypeStruct((b * s, DIFF_W), BF16),
        grid_spec=pltpu.PrefetchScalarGridSpec(
            num_scalar_prefetch=3,
            grid=(b, nb, nb),
            in_specs=[pl.BlockSpec((DIFF_W, blk), lambda bb, i, j, *_: (0, bb * nb + i)),
                      pl.BlockSpec((blk, DIFF_W), lambda *a: (kcol(*a), 0)),
                      pl.BlockSpec((DIFF_W, blk), lambda *a: (0, kcol(*a))),
                      pl.BlockSpec((1, blk), lambda bb, i, j, *_: (0, bb * nb + i)),
                      pl.BlockSpec((blk, 1), lambda *a: (kcol(*a), 0))]
            + [_full(a.shape) for a in params],
            out_specs=pl.BlockSpec((blk, DIFF_W), lambda bb, i, j, *_: (bb * nb + i, 0)),
            scratch_shapes=[pltpu.VMEM((DIFF_W, wide), BF16), pltpu.VMEM((blk, wide), F32),
                            pltpu.VMEM((blk, wide), BF16), pltpu.VMEM((1, wide), F32), pltpu.VMEM((1, wide), F32),
                            pltpu.VMEM((DIFF_W, 2 * blk), F32)]),
        compiler_params=_cparams(("parallel", "parallel", "arbitrary")),
        name="diff_attn",
    )(bmin, bmax, last, qt, k, vt, positions.reshape(1, b * s), positions.reshape(b * s, 1), *params)


def _cross_kv_body(m_ref, g_ref, wk_ref, wv_ref, gk_ref, bd_ref, k_ref, v_ref):
    hm = _rms(m_ref[...], g_ref[...]).astype(BF16)
    k = _dot(hm, wk_ref[...])
    k_ref[...] = (_seg_norm(k, bd_ref[...], CROSS_D) * gk_ref[...]).astype(k_ref.dtype)
    v_ref[...] = _dot(hm, wv_ref[...]).astype(v_ref.dtype)


def _cross_kv(mem, p):
    b, n, d = mem.shape
    params = [p["gmem"], p["wk"], p["wv"], p["gk"], p["bd64"]]
    return pl.pallas_call(
        _cross_kv_body,
        out_shape=[jax.ShapeDtypeStruct((b, n, CROSS_W), BF16)] * 2,
        grid=(b,),
        in_specs=[pl.BlockSpec((None, n, d), lambda i: (i, 0, 0))] + [_full(a.shape) for a in params],
        out_specs=[pl.BlockSpec((None, n, CROSS_W), lambda i: (i, 0, 0))] * 2,
        compiler_params=_cparams(("parallel",)),
        name="cross_kv",
    )(mem, *params)


def _merge_body(x_ref, yrec_ref, ymla_ref, ydiff_ref, kx_ref, vx_ref, gmix_ref, wg_ref, bg_ref,
                ps5_ref, pmla_ref, plru_ref, pdiff_ref, wout_ref,
                gcross_ref, wq_ref, gq_ref, bd_ref, wo_ref, o_ref):
    x = x_ref[...]
    d = x.shape[1]
    h = _rms(x, gmix_ref[...]).astype(BF16)
    branches = ((yrec_ref[:, 0:S5_W], ps5_ref), (ymla_ref[...], pmla_ref),
                (yrec_ref[:, S5_W:S5_W + LRU_W], plru_ref), (ydiff_ref[...], pdiff_ref))
    merged = jnp.zeros(x.shape, F32)
    for n, (y, p_ref) in enumerate(branches):
        gate = jax.nn.sigmoid(_dot(h, wg_ref[:, n * d:(n + 1) * d]) + bg_ref[:, n * d:(n + 1) * d])
        merged = merged + gate * _dot(y, p_ref[...])
    x1 = x + _dot(merged.astype(BF16), wout_ref[...])

    hc = _rms(x1, gcross_ref[...]).astype(BF16)
    q = _dot(hc, wq_ref[...])
    q = (_seg_norm(q, bd_ref[...], CROSS_D) * gq_ref[...]).astype(BF16)
    lane = lax.broadcasted_iota(I32, (1, CROSS_W), 1)
    kx = kx_ref[...]
    vx = vx_ref[...]
    o = jnp.zeros((x.shape[0], CROSS_W), F32)
    for hd in range(CROSS_H):
        head_lanes = lane // CROSS_D == hd
        s = _dot_t(jnp.where(head_lanes, q, jnp.zeros_like(q)), kx)
        p = jnp.exp(s - jnp.max(s, axis=-1, keepdims=True))
        p = p / jnp.sum(p, axis=-1, keepdims=True)
        o = o + _dot(p.astype(BF16), jnp.where(head_lanes, vx, jnp.zeros_like(vx)))
    o_ref[...] = x1 + _dot(o.astype(BF16), wo_ref[...])


def _merge(x, yrec, ymla, ydiff, kx, vx, p, seq):
    t, d = x.shape
    tm = _tile(seq, 512)
    per_b = seq // tm
    nm = kx.shape[1]
    params = [p["gmix"], p["wg"], p["bg"], p["ps5"], p["pmla"], p["plru"], p["pdiff"], p["wout"],
              p["gcross"], p["wq"], p["gq"], p["bd64"], p["wo"]]
    row = lambda w: pl.BlockSpec((tm, w), lambda i: (i, 0))
    mem_spec = pl.BlockSpec((None, nm, CROSS_W), lambda i: (i // per_b, 0, 0))
    return pl.pallas_call(
        _merge_body,
        out_shape=jax.ShapeDtypeStruct((t, d), F32),
        grid=(t // tm,),
        in_specs=[row(d), row(yrec.shape[1]), row(ymla.shape[1]), row(ydiff.shape[1]), mem_spec, mem_spec]
        + [_full(a.shape) for a in params],
        out_specs=row(d),
        compiler_params=_cparams(("parallel",), 0.85),
        name="merge_cross",
    )(x, yrec, ymla, ydiff, kx, vx, *params)


def _ffn_body(x_ref, g_ref, wg_ref, wu_ref, wd_ref, o_ref, h_scr, acc_scr):
    f = pl.program_id(1)

    @pl.when(f == 0)
    def _init():
        h_scr[...] = _rms(x_ref[...], g_ref[...]).astype(BF16)
        acc_scr[...] = jnp.zeros(acc_scr.shape, F32)

    h = h_scr[...]
    a = jax.nn.silu(_dot(h, wg_ref[...])) * _dot(h, wu_ref[...])
    acc_scr[...] += _dot(a.astype(BF16), wd_ref[...])

    @pl.when(f == pl.num_programs(1) - 1)
    def _fin():
        o_ref[...] = x_ref[...] + acc_scr[...]


def _ffn(x, g, wg, wu, wd):
    t, d = x.shape
    ff = wg.shape[1]
    tm = _tile(t, 1024)
    tf = _tile(ff, 256)
    return pl.pallas_call(
        _ffn_body,
        out_shape=jax.ShapeDtypeStruct((t, d), F32),
        grid=(t // tm, ff // tf),
        in_specs=[pl.BlockSpec((tm, d), lambda i, f: (i, 0)), _full(g.shape),
                  pl.BlockSpec((d, tf), lambda i, f: (0, f)), pl.BlockSpec((d, tf), lambda i, f: (0, f)),
                  pl.BlockSpec((tf, d), lambda i, f: (f, 0))],
        out_specs=pl.BlockSpec((tm, d), lambda i, f: (i, 0)),
        scratch_shapes=[pltpu.VMEM((tm, d), BF16), pltpu.VMEM((tm, d), F32)],
        compiler_params=_cparams(("parallel", "arbitrary")),
        name="ffn",
    )(x, g, wg, wu, wd)


def _router_body(x_ref, g_ref, wr_ref, h_ref, r_ref):
    hf = _rms(x_ref[...], g_ref[...])
    h_ref[...] = hf.astype(h_ref.dtype)
    logits = jnp.dot(hf, wr_ref[...], preferred_element_type=F32, precision=HI)
    lane = lax.broadcasted_iota(I32, logits.shape, 1).astype(F32)
    logits = jnp.where(lane < N_EXPERTS, logits, -jnp.inf)
    m1 = jnp.max(logits, axis=-1, keepdims=True)
    i1 = jnp.min(jnp.where(logits == m1, lane, float(LANE)), axis=-1, keepdims=True)
    rest = jnp.where(lane == i1, -jnp.inf, logits)
    m2 = jnp.max(rest, axis=-1, keepdims=True)
    i2 = jnp.min(jnp.where(rest == m2, lane, float(LANE)), axis=-1, keepdims=True)
    e2 = jnp.exp(m2 - m1)
    w1 = 1.0 / (1.0 + e2)
    w2 = e2 / (1.0 + e2)
    r_ref[...] = jnp.where(lane == 0, i1, jnp.where(lane == 1, i2, jnp.where(lane == 2, w1, jnp.where(lane == 3, w2, 0.0))))


def _router(x, g, w_router_pad):
    t, d = x.shape
    tm = _tile(t, 512)
    return pl.pallas_call(
        _router_body,
        out_shape=[jax.ShapeDtypeStruct((t, d), F32), jax.ShapeDtypeStruct((t, LANE), F32)],
        grid=(t // tm,),
        in_specs=[pl.BlockSpec((tm, d), lambda i: (i, 0)), _full(g.shape), _full(w_router_pad.shape)],
        out_specs=[pl.BlockSpec((tm, d), lambda i: (i, 0)), pl.BlockSpec((tm, LANE), lambda i: (i, 0))],
        compiler_params=_cparams(("parallel",)),
        name="moe_router",
    )(x, g, w_router_pad)


def _expert_body(be_ref, bv_ref, tok_ref, h_ref, rw_ref, wg_ref, wu_ref, wd_ref, o_ref,
                 xbuf, xb_scr, acc_scr, sem, *, rows):
    i, f = pl.program_id(0), pl.program_id(1)
    n_blk, nf = pl.num_programs(0), pl.num_programs(1)
    slot = i % 2

    def row_copy(blk, r, sl):
        return pltpu.make_async_copy(h_ref.at[pl.ds(tok_ref[blk * rows + r], 1)], xbuf.at[sl, pl.ds(r, 1)], sem.at[sl])

    def fetch(blk, sl):
        def body(r, c):
            row_copy(blk, r, sl).start()
            return c
        lax.fori_loop(0, rows, body, 0, unroll=8)

    def wait(blk, sl):
        def body(r, c):
            row_copy(blk, r, sl).wait()
            return c
        lax.fori_loop(0, rows, body, 0, unroll=8)

    @pl.when(f == 0)
    def _stage():
        acc_scr[...] = jnp.zeros(acc_scr.shape, F32)

        @pl.when(jnp.logical_and(i == 0, bv_ref[0] > 0))
        def _first():
            fetch(0, 0)

        nxt = jnp.minimum(i + 1, n_blk - 1)

        @pl.when(jnp.logical_and(i + 1 < n_blk, bv_ref[nxt] > 0))
        def _prefetch():
            fetch(nxt, 1 - slot)

        @pl.when(bv_ref[i] > 0)
        def _land():
            wait(i, slot)
            xb_scr[...] = xbuf[slot].astype(BF16)

    @pl.when(bv_ref[i] > 0)
    def _compute():
        x = xb_scr[...]
        a = jax.nn.silu(_dot(x, wg_ref[...])) * _dot(x, wu_ref[...])
        acc_scr[...] += _dot(a.astype(BF16), wd_ref[...])

    @pl.when(f == nf - 1)
    def _fin():
        o_ref[...] = acc_scr[...] * rw_ref[...]


def _experts(blk_e, blk_valid, row_tok, h, row_w, wg, wu, wd):
    d = h.shape[1]
    n_rows = row_tok.shape[0]
    ff = wg.shape[2]
    rows = MOE_ROWS
    tf = _tile(ff, 512)
    return pl.pallas_call(
        functools.partial(_expert_body, rows=rows),
        out_shape=jax.ShapeDtypeStruct((n_rows, d), F32),
        grid_spec=pltpu.PrefetchScalarGridSpec(
            num_scalar_prefetch=3,
            grid=(n_rows // rows, ff // tf),
            in_specs=[pl.BlockSpec(memory_space=pl.ANY),
                      pl.BlockSpec((rows, 1), lambda i, f, *_: (i, 0)),
                      pl.BlockSpec((None, d, tf), lambda i, f, be, *_: (be[i], 0, f)),
                      pl.BlockSpec((None, d, tf), lambda i, f, be, *_: (be[i], 0, f)),
                      pl.BlockSpec((None, tf, d), lambda i, f, be, *_: (be[i], f, 0))],
            out_specs=pl.BlockSpec((rows, d), lambda i, f, *_: (i, 0)),
            scratch_shapes=[pltpu.VMEM((2, rows, d), F32), pltpu.VMEM((rows, d), BF16),
                            pltpu.VMEM((rows, d), F32), pltpu.SemaphoreType.DMA((2,))]),
        compiler_params=_cparams(("arbitrary", "arbitrary")),
        name="moe_experts",
    )(blk_e, blk_valid, row_tok, h, row_w, wg, wu, wd)


def _combine_body(dest_ref, x_ref, ys_ref, o_ref, buf, sem, *, tm):
    base = pl.program_id(0) * tm

    def copy(r, k):
        return pltpu.make_async_copy(ys_ref.at[pl.ds(dest_ref[(base + r) * TOP_K + k], 1)],
                                     buf.at[k, pl.ds(r, 1)], sem)

    def start(r, c):
        for k in range(TOP_K):
            copy(r, k).start()
        return c

    def wait(r, c):
        for k in range(TOP_K):
            copy(r, k).wait()
        return c

    lax.fori_loop(0, tm, start, 0, unroll=8)
    lax.fori_loop(0, tm, wait, 0, unroll=8)
    o_ref[...] = x_ref[...] + sum(buf[k] for k in range(TOP_K))


def _combine(dest, x, ys):
    t, d = x.shape
    tm = _tile(t, 256)
    return pl.pallas_call(
        functools.partial(_combine_body, tm=tm),
        out_shape=jax.ShapeDtypeStruct((t, d), F32),
        grid_spec=pltpu.PrefetchScalarGridSpec(
            num_scalar_prefetch=1,
            grid=(t // tm,),
            in_specs=[pl.BlockSpec((tm, d), lambda i, de: (i, 0)), pl.BlockSpec(memory_space=pl.ANY)],
            out_specs=pl.BlockSpec((tm, d), lambda i, de: (i, 0)),
            scratch_shapes=[pltpu.VMEM((TOP_K, tm, d), F32), pltpu.SemaphoreType.DMA]),
        compiler_params=_cparams(("arbitrary",)),
        name="moe_combine",
    )(dest, x, ys)


def _moe(x, g, w_router, wg, wu, wd):
    t, d = x.shape
    n_assign = t * TOP_K
    rows = MOE_ROWS
    n_blk = -(-(n_assign + N_EXPERTS * (rows - 1)) // rows)
    n_rows = n_blk * rows
    wr_pad = jnp.zeros((d, LANE), F32).at[:, :N_EXPERTS].set(w_router)
    h, r = _router(x, g, wr_pad)
    top_e = r[:, 0:TOP_K].astype(I32)
    top_w = r[:, TOP_K:2 * TOP_K]
    flat_e = top_e.reshape(-1)
    onehot = (flat_e[:, None] == jnp.arange(N_EXPERTS, dtype=I32)[None, :]).astype(I32)
    csum = jnp.cumsum(onehot, axis=0)
    rank = jnp.sum(csum * onehot, axis=1) - 1
    counts = csum[-1]
    padded = (counts + rows - 1) // rows * rows
    pad_end = jnp.cumsum(padded)
    pad_start = pad_end - padded
    dest = (pad_start[flat_e] + rank).astype(I32)
    flat_t = jnp.repeat(jnp.arange(t, dtype=I32), TOP_K)
    row_tok = jnp.zeros((n_rows,), I32).at[dest].set(flat_t)
    row_w = jnp.zeros((n_rows,), F32).at[dest].set(top_w.reshape(-1))
    blk_start = jnp.arange(n_blk, dtype=I32) * rows
    blk_e = jnp.minimum(jnp.searchsorted(pad_end, blk_start, side="right"), N_EXPERTS - 1).astype(I32)
    blk_valid = (blk_start < pad_end[-1]).astype(I32)
    ys = _experts(blk_e, blk_valid, row_tok, h, row_w.reshape(n_rows, 1), wg, wu, wd)
    return _combine(dest, x, ys)


def _slot_pad(w, n_heads, width):
    lead = w.shape[:-1]
    w = w.reshape(lead + (n_heads, width))
    w = jnp.pad(w, [(0, 0)] * len(lead) + [(0, 0), (0, LANE - width)])
    return w.reshape(lead + (n_heads * LANE,))


def _rope_swap(w):
    half = MLA_ROPE // 2
    z = jnp.zeros(w.shape[:-1] + (MLA_NOPE,), w.dtype)
    tail = jnp.zeros(w.shape[:-1] + (LANE - MLA_QK,), w.dtype)
    return jnp.concatenate([z, w[..., MLA_NOPE + half:MLA_QK], w[..., MLA_NOPE:MLA_NOPE + half], tail], axis=-1)


def _block_diag(w):
    n, a, b = w.shape
    return jnp.einsum("nab,nm->namb", w, jnp.eye(n, dtype=w.dtype)).reshape(n * a, n * b)


def _rel_bucket_by_distance():
    n = np.arange(REL_MAX_DIST)
    exact = REL_BUCKETS // 2
    log_ratio = np.log(np.maximum(n, exact).astype(np.float32) / exact) / math.log(REL_MAX_DIST / exact)
    large = np.minimum(exact + (log_ratio * (REL_BUCKETS - exact)).astype(np.int32), REL_BUCKETS - 1)
    return np.where(n < exact, n, large).astype(np.int32)


def kernel(x, mem, positions, rel_table, g_mix, g_cross, g_mem, g_ffn, w_in, b_gate, s5_lam_re, s5_lam_im, s5_log_step, s5_b_re, s5_b_im, s5_c_re, s5_c_im, s5_d, s5_w_glu, s5_b_glu, mla_g_cq, mla_g_ckv, mla_w_uq, mla_w_ukv, mla_g_qn, mla_g_kn, lru_conv_w, lru_conv_b, lru_w_r, lru_b_r, lru_w_i, lru_b_i, lru_lam, diff_g_qn, diff_g_kn, diff_lq1, diff_lk1, diff_lq2, diff_lk2, diff_g_sub, w_branch, w_out, x_wq, x_wk, x_wv, x_wo, x_g_qn, x_g_kn, ffn_w_gate, ffn_w_up, ffn_w_down, moe_w_router, moe_w_gate, moe_w_up, moe_w_down):
    b, s, d = x.shape
    t = b * s
    depth = w_in.shape[0]
    positions = positions.astype(I32)
    pos_col = positions.reshape(t, 1)
    slots = -(-b // SUB) * SUB
    row2 = lambda a: a.reshape(1, -1).astype(F32)
    bd32 = jnp.asarray(np.kron(np.eye(DIFF_W // DIFF_D), np.ones((DIFF_D, DIFF_D))), F32)
    bd64 = jnp.asarray(np.kron(np.eye(CROSS_H), np.ones((CROSS_D, CROSS_D))), F32)
    half = MLA_ROPE // 2
    inv = ROPE_THETA ** (-np.arange(half, dtype=np.float32) / half)
    inv_full = np.zeros((1, LANE), np.float32)
    inv_full[0, MLA_NOPE:MLA_NOPE + half] = inv
    inv_full[0, MLA_NOPE + half:MLA_QK] = inv
    sgn = np.zeros((1, LANE), np.float32)
    sgn[0, MLA_NOPE:MLA_NOPE + half] = -1.0
    sgn[0, MLA_NOPE + half:MLA_QK] = 1.0
    bucket = _rel_bucket_by_distance()

    xf = x.reshape(t, d)
    for l in range(depth):
        lam_init = 0.8 - 0.6 * math.exp(-0.3 * l)
        w = w_in[l]
        off = np.cumsum([0, S5_W, MLA_QR, MLA_KVR, MLA_ROPE, LRU_W, LRU_W, DIFF_W, DIFF_W, DIFF_W])
        c_s5, c_cq, c_ckv, c_kpe, c_lx, c_lg, c_qd, c_kd, c_vd = (w[:, off[i]:off[i + 1]] for i in range(9))
        c_gate = w[:, off[9]:]
        zpad = lambda n: jnp.zeros((d, n), F32)
        kpe_pos = jnp.concatenate([zpad(MLA_NOPE), c_kpe, zpad(LANE - MLA_QK)], axis=1)
        kpe_sw = jnp.concatenate([zpad(MLA_NOPE), c_kpe[:, half:], c_kpe[:, :half], zpad(LANE - MLA_QK)], axis=1)
        w_rec = jnp.concatenate([c_s5, c_lx, c_lg], axis=1).astype(BF16)
        w_mla = jnp.concatenate([c_cq, zpad(2 * LANE - MLA_QR), c_ckv, kpe_pos, kpe_sw], axis=1).astype(BF16)
        w_diff = jnp.concatenate([c_qd, c_kd, c_vd], axis=1).astype(BF16)
        z_rec, z_mla, z_diff = _inproj(xf, row2(g_mix[l]), w_rec, w_mla, w_diff)

        dt = jnp.exp(s5_log_step[l])[:, None]
        lr, li = s5_lam_re[l], s5_lam_im[l]
        mag = jnp.exp(lr * dt)
        ar, ai = mag * jnp.cos(li * dt), mag * jnp.sin(li * dt)
        den = lr * lr + li * li
        zr = ((ar - 1.0) * lr + ai * li) / den
        zi = (ai * lr - (ar - 1.0) * li) / den
        bbr = zr[..., None] * s5_b_re[l] - zi[..., None] * s5_b_im[l]
        bbi = zr[..., None] * s5_b_im[l] + zi[..., None] * s5_b_re[l]
        rec_p = dict(
            ar=row2(ar), ai=row2(ai),
            bre=_block_diag(jnp.swapaxes(bbr, 1, 2)).astype(BF16), bim=_block_diag(jnp.swapaxes(bbi, 1, 2)).astype(BF16),
            cre=_block_diag(jnp.swapaxes(s5_c_re[l], 1, 2)).astype(BF16),
            cim=_block_diag(jnp.swapaxes(-s5_c_im[l], 1, 2)).astype(BF16),
            d=row2(s5_d[l]), wglu=s5_w_glu[l].astype(BF16), bglu=row2(s5_b_glu[l]),
            cw=lru_conv_w[l].astype(F32), cb=row2(lru_conv_b[l]),
            wr=_block_diag(lru_w_r[l]).astype(BF16), br=row2(lru_b_r[l]),
            wi=_block_diag(lru_w_i[l]).astype(BF16), bi=row2(lru_b_i[l]), lam=row2(lru_lam[l]))
        zt = jnp.swapaxes(z_rec.reshape(b, s, -1), 0, 1)
        zt = jnp.pad(zt, ((0, 0), (0, slots - b), (0, 0))).reshape(s * slots, -1)
        y_rec = _recurrent(zt, rec_p).reshape(s, slots, -1)[:, :b]
        y_rec = jnp.swapaxes(y_rec, 0, 1).reshape(t, -1)

        wuq = mla_w_uq[l].reshape(MLA_QR, MLA_H, MLA_QK)
        wukv = mla_w_ukv[l].reshape(MLA_KVR, MLA_H, MLA_NOPE + MLA_V)
        rows_pad = lambda a: jnp.pad(a, ((0, 2 * LANE - MLA_QR), (0, 0)))
        mla_p = dict(
            gcq=jnp.pad(row2(mla_g_cq[l]), ((0, 0), (0, 2 * LANE - MLA_QR))), gckv=row2(mla_g_ckv[l]),
            wq=rows_pad(_slot_pad(mla_w_uq[l], MLA_H, MLA_QK)).astype(BF16),
            wqs=rows_pad(_rope_swap(wuq).reshape(MLA_QR, MLA_H * LANE)).astype(BF16),
            wk=_slot_pad(wukv[..., :MLA_NOPE].reshape(MLA_KVR, -1), MLA_H, MLA_NOPE).astype(BF16),
            wv=wukv[..., MLA_NOPE:].reshape(MLA_KVR, -1).astype(BF16),
            gq=_slot_pad(row2(mla_g_qn[l]), 1, MLA_QK), gqs=_rope_swap(row2(mla_g_qn[l])),
            gk=_slot_pad(row2(mla_g_kn[l]), 1, MLA_QK), gks=_rope_swap(row2(mla_g_kn[l])),
            inv=jnp.asarray(inv_full), sgn=jnp.asarray(sgn))
        q_m, k_m, v_m = _mla_prep(z_mla, pos_col, mla_p)
        y_mla = _mla_attn(q_m, k_m, v_m, positions)

        diff_p = dict(
            gq=jnp.tile(row2(diff_g_qn[l]), (1, DIFF_W // DIFF_D)) * DIFF_D ** -0.5,
            gk=jnp.tile(row2(diff_g_kn[l]), (1, DIFF_W // DIFF_D)), bd32=bd32, bd64=bd64,
            tab=jnp.transpose(rel_table[bucket]).astype(F32),
            lqk=jnp.pad(jnp.stack([diff_lq1[l], diff_lk1[l], diff_lq2[l], diff_lk2[l]]).astype(F32),
                        ((0, SUB - 4), (0, LANE - DIFF_D))),
            gsub=jnp.tile(row2(diff_g_sub[l]), (1, DIFF_H)) * (1.0 - lam_init))
        q_d, k_d, v_d = _diff_prep(z_diff, diff_p)
        y_diff = _diff_attn(q_d, k_d, v_d, positions, diff_p, lam_init)

        cross_p = dict(gmem=row2(g_mem[l]), wk=x_wk[l].astype(BF16), wv=x_wv[l].astype(BF16),
                       gk=jnp.tile(row2(x_g_kn[l]), (1, CROSS_H)), bd64=bd64)
        kx, vx = _cross_kv(mem, cross_p)
        merge_p = dict(
            gmix=row2(g_mix[l]), wg=c_gate.astype(BF16), bg=row2(b_gate[l]),
            ps5=w_branch[l, 0].astype(BF16),
            pmla=w_branch[l, 1].astype(BF16),
            plru=w_branch[l, 2].astype(BF16), pdiff=w_branch[l, 3].astype(BF16), wout=w_out[l].astype(BF16),
            gcross=row2(g_cross[l]), wq=x_wq[l].astype(BF16),
            gq=jnp.tile(row2(x_g_qn[l]), (1, CROSS_H)) * CROSS_D ** -0.5, bd64=bd64, wo=x_wo[l].astype(BF16))
        xf = _merge(xf, y_rec, y_mla, y_diff, kx, vx, merge_p, s)

        if l % 2 == 0:
            e = l // 2
            xf = _ffn(xf, row2(g_ffn[l]), ffn_w_gate[e].astype(BF16), ffn_w_up[e].astype(BF16),
                      ffn_w_down[e].astype(BF16))
        else:
            e = l // 2
            xf = _moe(xf, row2(g_ffn[l]), moe_w_router[e], moe_w_gate[e].astype(BF16),
                      moe_w_up[e].astype(BF16), moe_w_down[e].astype(BF16))
    return xf.reshape(b, s, d)
```

```python
import functools
import math

import numpy as np
import jax
import jax.numpy as jnp
from jax import lax
from jax.experimental import pallas as pl
from jax.experimental.pallas import tpu as pltpu

F32 = jnp.float32
BF16 = jnp.bfloat16
I32 = jnp.int32
EPS = 1e-6
NEG = -1e30
HI = lax.Precision.HIGHEST

LANE = 128
SUB = 8
VMEM_BYTES = 64 * 1024 * 1024

S5_W, S5_GROUP, S5_STATE = 256, 16, 64
S5_GROUPS = S5_W // S5_GROUP
S5_N = S5_GROUPS * S5_STATE
MLA_H, MLA_QR, MLA_KVR, MLA_NOPE, MLA_ROPE, MLA_V = 4, 192, 128, 64, 32, 64
MLA_QK = MLA_NOPE + MLA_ROPE
ROPE_THETA = 10000.0
LRU_W, LRU_BLOCKS, LRU_CONV, LRU_C = 256, 4, 4, 8.0
DIFF_H, DIFF_D = 4, 32
DIFF_V = 2 * DIFF_D
DIFF_W = DIFF_H * DIFF_V
REL_BUCKETS, REL_MAX_DIST = 32, 128
CROSS_H, CROSS_D = 4, 64
CROSS_W = CROSS_H * CROSS_D
N_EXPERTS, TOP_K = 8, 2
N_BRANCH, BRANCH_W = 4, 256
MOE_ROWS = 512


def _cparams(sem, frac=0.75, flags=None):
    return pltpu.CompilerParams(dimension_semantics=sem, vmem_limit_bytes=int(VMEM_BYTES * frac), flags=flags)


def _tile(n, pref):
    t = min(n, pref)
    while n % t:
        t -= SUB
    return t


def _rms(x, g):
    return x * lax.rsqrt(jnp.mean(x * x, axis=-1, keepdims=True) + EPS) * g


def _full(shape):
    nd = len(shape)
    return pl.BlockSpec(shape, lambda *_: (0,) * nd)


def _dot(a, b):
    return jnp.dot(a, b, preferred_element_type=F32)


def _dot_t(a, b):
    return lax.dot_general(a, b, (((1,), (1,)), ((), ())), preferred_element_type=F32)


def _seg_norm(a, bd, width):
    ss = jnp.dot(a * a, bd, preferred_element_type=F32, precision=HI)
    return a * lax.rsqrt(ss * (1.0 / width) + EPS)


def _inproj_body(x_ref, g_ref, wr_ref, wm_ref, wd_ref, zr_ref, zm_ref, zd_ref):
    h = _rms(x_ref[...], g_ref[...]).astype(BF16)
    zr_ref[...] = _dot(h, wr_ref[...])
    zm_ref[...] = _dot(h, wm_ref[...])
    zd_ref[...] = _dot(h, wd_ref[...])


def _inproj(x, g, w_rec, w_mla, w_diff):
    t, d = x.shape
    tm = _tile(t, 512)
    outs = [jax.ShapeDtypeStruct((t, w.shape[1]), F32) for w in (w_rec, w_mla, w_diff)]
    return pl.pallas_call(
        _inproj_body,
        out_shape=outs,
        grid=(t // tm,),
        in_specs=[pl.BlockSpec((tm, d), lambda i: (i, 0)), _full(g.shape),
                  _full(w_rec.shape), _full(w_mla.shape), _full(w_diff.shape)],
        out_specs=[pl.BlockSpec((tm, w.shape[1]), lambda i: (i, 0)) for w in (w_rec, w_mla, w_diff)],
        compiler_params=_cparams(("parallel",)),
        name="inproj",
    )(x, g, w_rec, w_mla, w_diff)


def _recurrent_body(z_ref, ar_ref, ai_ref, bre_ref, bim_ref, cre_ref, cim_ref, d_ref, wglu_ref, bglu_ref,
                    cw_ref, cb_ref, wr_ref, br_ref, wi_ref, bi_ref, lam_ref,
                    o_ref, bur, bui, st_re, st_im, xbuf, a_scr, h_scr, st_lru, *, steps, slots, halo):
    rows = steps * SUB
    sub = SUB // slots
    row = lax.broadcasted_iota(I32, (SUB, 1), 0)

    @pl.when(pl.program_id(0) == 0)
    def _init():
        st_re[...] = jnp.zeros(st_re.shape, F32)
        st_im[...] = jnp.zeros(st_im.shape, F32)
        st_lru[...] = jnp.zeros(st_lru.shape, F32)
        xbuf[0:halo, :] = jnp.zeros((halo, LRU_W), F32)

    def shift_down(h):
        return h if sub == 1 else pltpu.roll(h, slots, axis=0)

    u = z_ref[:, 0:S5_W]
    ub = u.astype(BF16)
    bur[...] = _dot(ub, bre_ref[...])
    bui[...] = _dot(ub, bim_ref[...])
    ar = jnp.broadcast_to(ar_ref[...], (SUB, S5_N))
    ai = jnp.broadcast_to(ai_ref[...], (SUB, S5_N))

    def s5_step(t, carry):
        hr, hi = carry
        r = pl.multiple_of(t * SUB, SUB)
        br, bi = bur[pl.ds(r, SUB), :], bui[pl.ds(r, SUB), :]
        for k in range(sub):
            pr, pi = shift_down(hr), shift_down(hi)
            nr = ar * pr - ai * pi + br
            ni = ar * pi + ai * pr + bi
            hr = nr if k == 0 else jnp.where(row >= k * slots, nr, hr)
            hi = ni if k == 0 else jnp.where(row >= k * slots, ni, hi)
        bur[pl.ds(r, SUB), :] = hr
        bui[pl.ds(r, SUB), :] = hi
        return hr, hi

    hr, hi = lax.fori_loop(0, steps, s5_step, (st_re[...], st_im[...]))
    st_re[...] = hr
    st_im[...] = hi
    y = _dot(bur[...].astype(BF16), cre_ref[...]) + _dot(bui[...].astype(BF16), cim_ref[...]) + d_ref[...] * u
    y = jax.nn.gelu(y)
    y = y * jax.nn.sigmoid(_dot(y.astype(BF16), wglu_ref[...]) + bglu_ref[...])
    o_ref[:, 0:S5_W] = y.astype(o_ref.dtype)

    xbuf[halo:halo + rows, :] = z_ref[:, S5_W:S5_W + LRU_W]
    taps = [halo - (LRU_CONV - 1 - k) * slots for k in range(LRU_CONV)]
    xc = cb_ref[...] + sum(xbuf[o:o + rows, :] * cw_ref[k:k + 1, :] for k, o in enumerate(taps))
    xbuf[0:halo, :] = xbuf[rows:rows + halo, :]
    xcb = xc.astype(BF16)
    rg = jax.nn.sigmoid(_dot(xcb, wr_ref[...]) + br_ref[...])
    ig = jax.nn.sigmoid(_dot(xcb, wi_ref[...]) + bi_ref[...])
    nl = -lam_ref[...]
    softplus = jnp.maximum(nl, 0.0) + jnp.log1p(jnp.exp(-jnp.abs(nl)))
    log_a = -LRU_C * rg * softplus
    a_scr[...] = jnp.exp(log_a)
    th = jnp.tanh(log_a)
    h_scr[...] = jnp.sqrt(-2.0 * th / (1.0 - th)) * (ig * xc)

    def lru_step(t, h):
        r = pl.multiple_of(t * SUB, SUB)
        a, inp = a_scr[pl.ds(r, SUB), :], h_scr[pl.ds(r, SUB), :]
        for k in range(sub):
            nh = a * shift_down(h) + inp
            h = nh if k == 0 else jnp.where(row >= k * slots, nh, h)
        h_scr[pl.ds(r, SUB), :] = h
        return h

    st_lru[...] = lax.fori_loop(0, steps, lru_step, st_lru[...])
    gate = z_ref[:, S5_W + LRU_W:S5_W + 2 * LRU_W]
    o_ref[:, S5_W:S5_W + LRU_W] = (h_scr[...] * jax.nn.gelu(gate)).astype(o_ref.dtype)


def _recurrent(z, p, slots):
    rows_total = z.shape[0]
    steps = _tile(rows_total // SUB, 64)
    rows = steps * SUB
    halo = -(-(LRU_CONV - 1) * slots // SUB) * SUB
    params = [p["ar"], p["ai"], p["bre"], p["bim"], p["cre"], p["cim"], p["d"], p["wglu"], p["bglu"],
              p["cw"], p["cb"], p["wr"], p["br"], p["wi"], p["bi"], p["lam"]]
    return pl.pallas_call(
        functools.partial(_recurrent_body, steps=steps, slots=slots, halo=halo),
        out_shape=jax.ShapeDtypeStruct((rows_total, S5_W + LRU_W), BF16),
        grid=(rows_total // rows,),
        in_specs=[pl.BlockSpec((rows, z.shape[1]), lambda i: (i, 0))] + [_full(a.shape) for a in params],
        out_specs=pl.BlockSpec((rows, S5_W + LRU_W), lambda i: (i, 0)),
        scratch_shapes=[pltpu.VMEM((rows, S5_N), F32), pltpu.VMEM((rows, S5_N), F32),
                        pltpu.VMEM((SUB, S5_N), F32), pltpu.VMEM((SUB, S5_N), F32),
                        pltpu.VMEM((rows + halo, LRU_W), F32), pltpu.VMEM((rows, LRU_W), F32),
                        pltpu.VMEM((rows, LRU_W), F32), pltpu.VMEM((SUB, LRU_W), F32)],
        compiler_params=_cparams(("arbitrary",)),
        name="recurrent",
    )(z, *params)


def _mla_prep_body(z_ref, pos_ref, gcq_ref, gckv_ref, wq_ref, wqs_ref, wk_ref, wv_ref,
                   gq_ref, gqs_ref, gk_ref, gks_ref, inv_ref, sgn_ref, qt_ref, k_ref, vt_ref):
    cq = z_ref[:, 0:2 * LANE]
    ckv = z_ref[:, 2 * LANE:3 * LANE]
    kpe = z_ref[:, 3 * LANE:4 * LANE]
    kpe_sw = z_ref[:, 4 * LANE:5 * LANE]
    hq = (cq * lax.rsqrt(jnp.sum(cq * cq, axis=-1, keepdims=True) * (1.0 / MLA_QR) + EPS) * gcq_ref[...]).astype(BF16)
    hkv = _rms(ckv, gckv_ref[...]).astype(BF16)
    q = _dot(hq, wq_ref[...])
    qs = _dot(hq, wqs_ref[...])
    kn = _dot(hkv, wk_ref[...])
    v = _dot(hkv, wv_ref[...])
    blk = vt_ref.shape[2]
    for r in range(vt_ref.shape[0]):
        vt_ref[r] = v[r * blk:(r + 1) * blk, :].T.astype(vt_ref.dtype)
    ang = pos_ref[...].astype(F32) * inv_ref[...]
    cos = jnp.cos(ang)
    sin = jnp.sin(ang) * sgn_ref[...]
    qc, qsn = gq_ref[...] * cos * (MLA_QK ** -0.5), gqs_ref[...] * sin * (MLA_QK ** -0.5)
    kc, ksn = gk_ref[...] * cos, gks_ref[...] * sin
    for h in range(MLA_H):
        sl = slice(h * LANE, (h + 1) * LANE)
        qh = q[:, sl]
        r = lax.rsqrt(jnp.sum(qh * qh, axis=-1, keepdims=True) * (1.0 / MLA_QK) + EPS)
        qt_ref[sl, :] = ((qh * qc + qs[:, sl] * qsn) * r).T.astype(qt_ref.dtype)
        kh = kn[:, sl] + kpe
        r = lax.rsqrt(jnp.sum(kh * kh, axis=-1, keepdims=True) * (1.0 / MLA_QK) + EPS)
        k_ref[:, sl] = ((kh * kc + kpe_sw * ksn) * r).astype(k_ref.dtype)


def _mla_prep(z, pos_col, p, blk):
    t = z.shape[0]
    tm = max(_tile(t, 512), blk)
    params = [p["gcq"], p["gckv"], p["wq"], p["wqs"], p["wk"], p["wv"],
              p["gq"], p["gqs"], p["gk"], p["gks"], p["inv"], p["sgn"]]
    width = MLA_H * LANE
    vw = MLA_H * MLA_V
    return pl.pallas_call(
        _mla_prep_body,
        out_shape=[jax.ShapeDtypeStruct((width, t), BF16), jax.ShapeDtypeStruct((t, width), BF16),
                   jax.ShapeDtypeStruct((t // blk, vw, blk), BF16)],
        grid=(t // tm,),
        in_specs=[pl.BlockSpec((tm, z.shape[1]), lambda i: (i, 0)), pl.BlockSpec((tm, 1), lambda i: (i, 0))]
        + [_full(a.shape) for a in params],
        out_specs=[pl.BlockSpec((width, tm), lambda i: (0, i)), pl.BlockSpec((tm, width), lambda i: (i, 0)),
                   pl.BlockSpec((tm // blk, vw, blk), lambda i: (i, 0, 0))],
        compiler_params=_cparams(("parallel",)),
        name="mla_prep",
    )(z, pos_col, *params)


def _block_tables(positions, blk):
    b, s = positions.shape
    nb = s // blk
    pb = positions.reshape(b, nb, blk)
    bmin = pb.min(-1)
    bmax = pb.max(-1)
    live = bmax[:, :, None] >= bmin[:, None, :]
    last = jnp.max(jnp.where(live, jnp.arange(nb, dtype=I32)[None, None, :], 0), axis=-1)
    return bmin.reshape(-1).astype(I32), bmax.reshape(-1).astype(I32), last.reshape(-1).astype(I32)


def _mla_attn_body(bmin_ref, bmax_ref, last_ref, qt_ref, k_ref, vt_ref, pq_ref, pk_ref, o_ref,
                   s_scr, p_scr, m_scr, l_scr, acc_scr, *, nb):
    b, i = pl.program_id(0), pl.program_id(1)
    qi = b * nb + i
    tq = qt_ref.shape[1]
    tk = vt_ref.shape[2]
    m_scr[...] = jnp.full(m_scr.shape, -jnp.inf, F32)
    l_scr[...] = jnp.zeros(l_scr.shape, F32)
    acc_scr[...] = jnp.zeros(acc_scr.shape, F32)

    def score(j, slot):
        r = pl.multiple_of(j * tk, tk)
        for h in range(MLA_H):
            s_scr[slot, :, h * tq:(h + 1) * tq] = _dot(k_ref[pl.ds(r, tk), h * LANE:(h + 1) * LANE],
                                                       qt_ref[h * LANE:(h + 1) * LANE, :])

    def reduce(j, slot, masked):
        if masked:
            r = pl.multiple_of(j * tk, tk)
            mask = pq_ref[...] >= pk_ref[pl.ds(r, tk), :]
            for h in range(MLA_H):
                cols = slice(h * tq, (h + 1) * tq)
                s_scr[slot, :, cols] = jnp.where(mask, s_scr[slot, :, cols], NEG)
        s = s_scr[slot]
        m_prev = m_scr[...]
        m_new = jnp.maximum(m_prev, jnp.max(s, axis=0, keepdims=True))
        alpha = jnp.exp(m_prev - m_new)
        p = jnp.exp(s - m_new)
        l_scr[...] = alpha * l_scr[...] + jnp.sum(p, axis=0, keepdims=True)
        m_scr[...] = m_new
        p_scr[...] = p.astype(BF16)
        for h in range(MLA_H):
            rows = slice(h * MLA_V, (h + 1) * MLA_V)
            cols = slice(h * tq, (h + 1) * tq)
            acc_scr[rows, :] = alpha[:, cols] * acc_scr[rows, :] + _dot(vt_ref[j, rows, :], p_scr[:, cols])

    last = last_ref[qi]

    def step(j, carry):
        kj = b * nb + j
        live = bmax_ref[qi] >= bmin_ref[kj]
        unmasked = bmin_ref[qi] >= bmax_ref[kj]

        @pl.when(unmasked)
        def _past():
            score(j, 0)
            reduce(j, 0, False)

        @pl.when(jnp.logical_and(live, jnp.logical_not(unmasked)))
        def _diag():
            score(j, 0)
            reduce(j, 0, True)

        return carry

    lax.fori_loop(0, last + 1, step, 0)
    for h in range(MLA_H):
        rows = slice(h * MLA_V, (h + 1) * MLA_V)
        acc_scr[rows, :] = acc_scr[rows, :] / l_scr[:, h * tq:(h + 1) * tq]
    o_ref[...] = acc_scr[...].T.astype(o_ref.dtype)


def _mla_attn(qt, k, vt, positions):
    b, s = positions.shape
    width = k.shape[-1]
    _, vw, blk = vt.shape
    nb = s // blk
    wide = MLA_H * blk
    bmin, bmax, last = _block_tables(positions, blk)
    return pl.pallas_call(
        functools.partial(_mla_attn_body, nb=nb),
        out_shape=jax.ShapeDtypeStruct((b * s, vw), BF16),
        grid_spec=pltpu.PrefetchScalarGridSpec(
            num_scalar_prefetch=3,
            grid=(b, nb),
            in_specs=[pl.BlockSpec((width, blk), lambda bb, i, *_: (0, bb * nb + i)),
                      pl.BlockSpec((s, width), lambda bb, i, *_: (bb, 0)),
                      pl.BlockSpec((nb, vw, blk), lambda bb, i, *_: (bb, 0, 0)),
                      pl.BlockSpec((1, blk), lambda bb, i, *_: (0, bb * nb + i)),
                      pl.BlockSpec((s, 1), lambda bb, i, *_: (bb, 0))],
            out_specs=pl.BlockSpec((blk, vw), lambda bb, i, *_: (bb * nb + i, 0)),
            scratch_shapes=[pltpu.VMEM((1, blk, wide), F32), pltpu.VMEM((blk, wide), BF16),
                            pltpu.VMEM((1, wide), F32), pltpu.VMEM((1, wide), F32), pltpu.VMEM((vw, blk), F32)]),
        compiler_params=_cparams(("parallel", "parallel")),
        name="mla_attn",
    )(bmin, bmax, last, qt, k, vt, positions.reshape(1, b * s), positions.reshape(b * s, 1))


def _diff_prep_body(z_ref, gq_ref, gk_ref, bd_ref, qt_ref, k_ref, vt_ref):
    bd = bd_ref[...]
    qt_ref[...] = (_seg_norm(z_ref[:, 0:DIFF_W], bd, DIFF_D) * gq_ref[...]).T.astype(qt_ref.dtype)
    k_ref[...] = (_seg_norm(z_ref[:, DIFF_W:2 * DIFF_W], bd, DIFF_D) * gk_ref[...]).astype(k_ref.dtype)
    blk = vt_ref.shape[2]
    for r in range(vt_ref.shape[0]):
        vt_ref[r] = z_ref[r * blk:(r + 1) * blk, 2 * DIFF_W:3 * DIFF_W].T.astype(vt_ref.dtype)


def _diff_prep(z, p, blk):
    t = z.shape[0]
    tm = max(_tile(t, 512), blk)
    params = [p["gq"], p["gk"], p["bd32"]]
    row = pl.BlockSpec((tm, DIFF_W), lambda i: (i, 0))
    col = pl.BlockSpec((DIFF_W, tm), lambda i: (0, i))
    return pl.pallas_call(
        _diff_prep_body,
        out_shape=[jax.ShapeDtypeStruct((DIFF_W, t), BF16), jax.ShapeDtypeStruct((t, DIFF_W), BF16),
                   jax.ShapeDtypeStruct((t // blk, DIFF_W, blk), BF16)],
        grid=(t // tm,),
        in_specs=[pl.BlockSpec((tm, z.shape[1]), lambda i: (i, 0))] + [_full(a.shape) for a in params],
        out_specs=[col, row, pl.BlockSpec((tm // blk, DIFF_W, blk), lambda i: (i, 0, 0))],
        compiler_params=_cparams(("parallel",)),
        name="diff_prep",
    )(z, *params)


def _diff_attn_body(bmin_ref, bmax_ref, last_ref, qt_ref, k_ref, vt_ref, pq_ref, pk_ref, tab_ref, lqk_ref,
                    gsub_ref, o_ref, qm_scr, s_scr, p_scr, m_scr, l_scr, acc_scr, *, nb, lam_init):
    b, i = pl.program_id(0), pl.program_id(1)
    qi = b * nb + i
    tq = qt_ref.shape[1]
    tk = vt_ref.shape[2]
    n_streams = 2 * DIFF_H
    m_scr[...] = jnp.full(m_scr.shape, -jnp.inf, F32)
    l_scr[...] = jnp.zeros(l_scr.shape, F32)
    acc_scr[...] = jnp.zeros(acc_scr.shape, F32)
    qv = qt_ref[...]
    feat = lax.broadcasted_iota(I32, (DIFF_W, 1), 0)
    for c in range(n_streams):
        qm_scr[:, c * tq:(c + 1) * tq] = jnp.where(feat // DIFF_D == c, qv, jnp.zeros_like(qv))

    def score(j, slot):
        r = pl.multiple_of(j * tk, tk)
        s_scr[slot] = _dot(k_ref[pl.ds(r, tk), :], qm_scr[...])

    def reduce(j, slot, masked, far):
        if not far:
            r = pl.multiple_of(j * tk, tk)
            dist = pq_ref[...] - pk_ref[pl.ds(r, tk), :]
            n = jnp.clip(dist, 0, REL_MAX_DIST - 1)
        for h in range(DIFF_H):
            cols = slice(2 * h * tq, (2 * h + 2) * tq)
            if far:
                bias = tab_ref[h:h + 1, REL_MAX_DIST - 1:REL_MAX_DIST]
            else:
                trow = jnp.broadcast_to(tab_ref[h:h + 1, :], (tk, LANE))
                bias = jnp.concatenate(
                    [jnp.take_along_axis(trow, n[:, c * LANE:(c + 1) * LANE], axis=1) for c in range(tq // LANE)],
                    axis=1)
                bias = jnp.concatenate([bias, bias], axis=1)
            sh = s_scr[slot, :, cols] + bias
            if masked:
                mask = dist >= 0
                sh = jnp.where(jnp.concatenate([mask, mask], axis=1), sh, NEG)
            s_scr[slot, :, cols] = sh
        s = s_scr[slot]
        m_prev = m_scr[...]
        m_new = jnp.maximum(m_prev, jnp.max(s, axis=0, keepdims=True))
        alpha = jnp.exp(m_prev - m_new)
        p = jnp.exp(s - m_new)
        l_scr[...] = alpha * l_scr[...] + jnp.sum(p, axis=0, keepdims=True)
        m_scr[...] = m_new
        p_scr[...] = p.astype(BF16)
        for h in range(DIFF_H):
            rows = slice(h * DIFF_V, (h + 1) * DIFF_V)
            cols = slice(2 * h * tq, (2 * h + 2) * tq)
            acc_scr[rows, :] = alpha[:, cols] * acc_scr[rows, :] + _dot(vt_ref[j, rows, :], p_scr[:, cols])

    last = last_ref[qi]

    def step(j, carry):
        kj = b * nb + j
        live = bmax_ref[qi] >= bmin_ref[kj]
        gap = bmin_ref[qi] - bmax_ref[kj]

        @pl.when(jnp.logical_and(live, gap < 0))
        def _diag():
            score(j, 0)
            reduce(j, 0, True, False)

        @pl.when(jnp.logical_and(gap >= 0, gap < REL_MAX_DIST - 1))
        def _near():
            score(j, 0)
            reduce(j, 0, False, False)

        @pl.when(gap >= REL_MAX_DIST - 1)
        def _far():
            score(j, 0)
            reduce(j, 0, False, True)

        return carry

    lax.fori_loop(0, last + 1, step, 0)
    lq = lqk_ref[...]
    lam = (jnp.exp(jnp.sum(lq[0:1] * lq[1:2], axis=-1, keepdims=True))
           - jnp.exp(jnp.sum(lq[2:3] * lq[3:4], axis=-1, keepdims=True)) + lam_init)
    for h in range(DIFF_H):
        rows = slice(h * DIFF_V, (h + 1) * DIFF_V)
        c1 = slice(2 * h * tq, (2 * h + 1) * tq)
        c2 = slice((2 * h + 1) * tq, (2 * h + 2) * tq)
        out = (acc_scr[rows, 0:tq] / l_scr[:, c1] - lam * (acc_scr[rows, tq:2 * tq] / l_scr[:, c2]))
        ms = jnp.sum(out * out, axis=0, keepdims=True) * (1.0 / DIFF_V)
        acc_scr[rows, 0:tq] = out * lax.rsqrt(ms + EPS) * gsub_ref[rows, :]
    o_ref[...] = acc_scr[:, 0:tq].T.astype(o_ref.dtype)


def _diff_attn(qt, k, vt, positions, p, lam_init):
    b, s = positions.shape
    blk = vt.shape[2]
    nb = s // blk
    wide = 2 * DIFF_H * blk
    bmin, bmax, last = _block_tables(positions, blk)
    gsub = jnp.broadcast_to(p["gsub"].reshape(DIFF_W, 1), (DIFF_W, blk))
    params = [p["tab"], p["lqk"], gsub]
    return pl.pallas_call(
        functools.partial(_diff_attn_body, nb=nb, lam_init=lam_init),
        out_shape=jax.ShapeDtypeStruct((b * s, DIFF_W), BF16),
        grid_spec=pltpu.PrefetchScalarGridSpec(
            num_scalar_prefetch=3,
            grid=(b, nb),
            in_specs=[pl.BlockSpec((DIFF_W, blk), lambda bb, i, *_: (0, bb * nb + i)),
                      pl.BlockSpec((s, DIFF_W), lambda bb, i, *_: (bb, 0)),
                      pl.BlockSpec((nb, DIFF_W, blk), lambda bb, i, *_: (bb, 0, 0)),
                      pl.BlockSpec((1, blk), lambda bb, i, *_: (0, bb * nb + i)),
                      pl.BlockSpec((s, 1), lambda bb, i, *_: (bb, 0))]
            + [_full(a.shape) for a in params],
            out_specs=pl.BlockSpec((blk, DIFF_W), lambda bb, i, *_: (bb * nb + i, 0)),
            scratch_shapes=[pltpu.VMEM((DIFF_W, wide), BF16), pltpu.VMEM((1, blk, wide), F32),
                            pltpu.VMEM((blk, wide), BF16), pltpu.VMEM((1, wide), F32), pltpu.VMEM((1, wide), F32),
                            pltpu.VMEM((DIFF_W, 2 * blk), F32)]),
        compiler_params=_cparams(("parallel", "parallel")),
        name="diff_attn",
    )(bmin, bmax, last, qt, k, vt, positions.reshape(1, b * s), positions.reshape(b * s, 1), *params)


def _cross_kv_body(m_ref, g_ref, wk_ref, wv_ref, gk_ref, bd_ref, k_ref, v_ref):
    hm = _rms(m_ref[...], g_ref[...]).astype(BF16)
    k = _dot(hm, wk_ref[...])
    k_ref[...] = (_seg_norm(k, bd_ref[...], CROSS_D) * gk_ref[...]).astype(k_ref.dtype)
    v_ref[...] = _dot(hm, wv_ref[...]).astype(v_ref.dtype)


def _cross_kv(mem, p):
    b, n, d = mem.shape
    params = [p["gmem"], p["wk"], p["wv"], p["gk"], p["bd64"]]
    return pl.pallas_call(
        _cross_kv_body,
        out_shape=[jax.ShapeDtypeStruct((b, n, CROSS_W), BF16)] * 2,
        grid=(b,),
        in_specs=[pl.BlockSpec((None, n, d), lambda i: (i, 0, 0))] + [_full(a.shape) for a in params],
        out_specs=[pl.BlockSpec((None, n, CROSS_W), lambda i: (i, 0, 0))] * 2,
        compiler_params=_cparams(("parallel",)),
        name="cross_kv",
    )(mem, *params)


def _merge_body(x_ref, yrec_ref, ymla_ref, ydiff_ref, kx_ref, vx_ref, gmix_ref, wg_ref, bg_ref,
                ps5_ref, pmla_ref, plru_ref, pdiff_ref, wout_ref,
                gcross_ref, wq_ref, gq_ref, bd_ref, wo_ref, o_ref):
    x = x_ref[...]
    d = x.shape[1]
    h = _rms(x, gmix_ref[...]).astype(BF16)
    branches = ((yrec_ref[:, 0:S5_W], ps5_ref), (ymla_ref[...], pmla_ref),
                (yrec_ref[:, S5_W:S5_W + LRU_W], plru_ref), (ydiff_ref[...], pdiff_ref))
    merged = jnp.zeros(x.shape, F32)
    for n, (y, p_ref) in enumerate(branches):
        gate = jax.nn.sigmoid(_dot(h, wg_ref[:, n * d:(n + 1) * d]) + bg_ref[:, n * d:(n + 1) * d])
        merged = merged + gate * _dot(y, p_ref[...])
    x1 = x + _dot(merged.astype(BF16), wout_ref[...])

    hc = _rms(x1, gcross_ref[...]).astype(BF16)
    q = _dot(hc, wq_ref[...])
    q = (_seg_norm(q, bd_ref[...], CROSS_D) * gq_ref[...]).astype(BF16)
    lane = lax.broadcasted_iota(I32, (1, CROSS_W), 1)
    kx = kx_ref[...]
    vx = vx_ref[...]
    o = jnp.zeros((x.shape[0], CROSS_W), F32)
    for hd in range(CROSS_H):
        head_lanes = lane // CROSS_D == hd
        s = _dot_t(jnp.where(head_lanes, q, jnp.zeros_like(q)), kx)
        p = jnp.exp(s - jnp.max(s, axis=-1, keepdims=True))
        p = p / jnp.sum(p, axis=-1, keepdims=True)
        o = o + _dot(p.astype(BF16), jnp.where(head_lanes, vx, jnp.zeros_like(vx)))
    o_ref[...] = x1 + _dot(o.astype(BF16), wo_ref[...])


def _merge(x, yrec, ymla, ydiff, kx, vx, p, seq):
    t, d = x.shape
    tm = _tile(seq, 512)
    per_b = seq // tm
    nm = kx.shape[1]
    params = [p["gmix"], p["wg"], p["bg"], p["ps5"], p["pmla"], p["plru"], p["pdiff"], p["wout"],
              p["gcross"], p["wq"], p["gq"], p["bd64"], p["wo"]]
    row = lambda w: pl.BlockSpec((tm, w), lambda i: (i, 0))
    mem_spec = pl.BlockSpec((None, nm, CROSS_W), lambda i: (i // per_b, 0, 0))
    return pl.pallas_call(
        _merge_body,
        out_shape=jax.ShapeDtypeStruct((t, d), F32),
        grid=(t // tm,),
        in_specs=[row(d), row(yrec.shape[1]), row(ymla.shape[1]), row(ydiff.shape[1]), mem_spec, mem_spec]
        + [_full(a.shape) for a in params],
        out_specs=row(d),
        compiler_params=_cparams(("parallel",), 0.85),
        name="merge_cross",
    )(x, yrec, ymla, ydiff, kx, vx, *params)


def _ffn_body(x_ref, g_ref, wg_ref, wu_ref, wd_ref, o_ref, h_scr, acc_scr):
    f = pl.program_id(1)

    @pl.when(f == 0)
    def _init():
        h_scr[...] = _rms(x_ref[...], g_ref[...]).astype(BF16)
        acc_scr[...] = jnp.zeros(acc_scr.shape, F32)

    h = h_scr[...]
    a = jax.nn.silu(_dot(h, wg_ref[...])) * _dot(h, wu_ref[...])
    acc_scr[...] += _dot(a.astype(BF16), wd_ref[...])

    @pl.when(f == pl.num_programs(1) - 1)
    def _fin():
        o_ref[...] = x_ref[...] + acc_scr[...]


def _ffn(x, g, wg, wu, wd):
    t, d = x.shape
    ff = wg.shape[1]
    tm = _tile(t, 1024)
    tf = _tile(ff, 256)
    return pl.pallas_call(
        _ffn_body,
        out_shape=jax.ShapeDtypeStruct((t, d), F32),
        grid=(t // tm, ff // tf),
        in_specs=[pl.BlockSpec((tm, d), lambda i, f: (i, 0)), _full(g.shape),
                  pl.BlockSpec((d, tf), lambda i, f: (0, f)), pl.BlockSpec((d, tf), lambda i, f: (0, f)),
                  pl.BlockSpec((tf, d), lambda i, f: (f, 0))],
        out_specs=pl.BlockSpec((tm, d), lambda i, f: (i, 0)),
        scratch_shapes=[pltpu.VMEM((tm, d), BF16), pltpu.VMEM((tm, d), F32)],
        compiler_params=_cparams(("parallel", "arbitrary")),
        name="ffn",
    )(x, g, wg, wu, wd)


def _router_body(x_ref, g_ref, wr_ref, r_ref):
    hf = _rms(x_ref[...], g_ref[...])
    logits = jnp.dot(hf, wr_ref[...], preferred_element_type=F32, precision=HI)
    lane = lax.broadcasted_iota(I32, logits.shape, 1).astype(F32)
    logits = jnp.where(lane < N_EXPERTS, logits, -jnp.inf)
    m1 = jnp.max(logits, axis=-1, keepdims=True)
    i1 = jnp.min(jnp.where(logits == m1, lane, float(LANE)), axis=-1, keepdims=True)
    rest = jnp.where(lane == i1, -jnp.inf, logits)
    m2 = jnp.max(rest, axis=-1, keepdims=True)
    i2 = jnp.min(jnp.where(rest == m2, lane, float(LANE)), axis=-1, keepdims=True)
    e2 = jnp.exp(m2 - m1)
    w1 = 1.0 / (1.0 + e2)
    w2 = e2 / (1.0 + e2)
    r_ref[...] = jnp.where(lane == 0, i1, jnp.where(lane == 1, i2, jnp.where(lane == 2, w1, jnp.where(lane == 3, w2, 0.0))))


def _router(x, g, w_router_pad):
    t, d = x.shape
    tm = _tile(t, 512)
    return pl.pallas_call(
        _router_body,
        out_shape=jax.ShapeDtypeStruct((t, LANE), F32),
        grid=(t // tm,),
        in_specs=[pl.BlockSpec((tm, d), lambda i: (i, 0)), _full(g.shape), _full(w_router_pad.shape)],
        out_specs=pl.BlockSpec((tm, LANE), lambda i: (i, 0)),
        compiler_params=_cparams(("parallel",)),
        name="moe_router",
    )(x, g, w_router_pad)


def _expert_body(be_ref, bv_ref, tok_ref, x_ref, g_ref, wg_ref, wu_ref, wd_ref, o_ref,
                 xbuf, xb_scr, acc_scr, sem, *, rows, nf):
    i, f = pl.program_id(0), pl.program_id(1)
    n_blk = pl.num_programs(0)
    slot = i % 2
    per = rows // nf

    def row_copy(blk, r, sl):
        return pltpu.make_async_copy(x_ref.at[pl.ds(tok_ref[blk * rows + r], 1)], xbuf.at[sl, pl.ds(r, 1)], sem.at[sl])

    @pl.when(jnp.logical_and(jnp.logical_and(i == 0, f == 0), bv_ref[0] > 0))
    def _first():
        def body(r, c):
            row_copy(0, r, 0).start()
            return c
        lax.fori_loop(0, rows, body, 0, unroll=8)

    @pl.when(f == 0)
    def _stage():
        acc_scr[...] = jnp.zeros(acc_scr.shape, F32)

        @pl.when(bv_ref[i] > 0)
        def _land():
            def body(r, c):
                row_copy(i, r, slot).wait()
                return c
            lax.fori_loop(0, rows, body, 0, unroll=8)
            xb_scr[...] = _rms(xbuf[slot], g_ref[...]).astype(BF16)

    def compute():
        x = xb_scr[...]
        a = jax.nn.silu(_dot(x, wg_ref[...])) * _dot(x, wu_ref[...])
        acc_scr[...] += _dot(a.astype(BF16), wd_ref[...])

    nxt = jnp.minimum(i + 1, n_blk - 1)
    has_next = jnp.logical_and(i + 1 < n_blk, bv_ref[nxt] > 0)

    @pl.when(has_next)
    def _compute_and_prefetch():
        for r in range(per):
            row_copy(nxt, f * per + r, 1 - slot).start()
        compute()

    @pl.when(jnp.logical_and(bv_ref[i] > 0, jnp.logical_not(has_next)))
    def _compute_only():
        compute()

    @pl.when(f == nf - 1)
    def _fin():
        o_ref[...] = acc_scr[...]


def _experts(blk_e, blk_valid, row_tok, x, g, wg, wu, wd):
    d = x.shape[1]
    n_rows = row_tok.shape[0]
    ff = wg.shape[2]
    rows = MOE_ROWS
    tf = _tile(ff, 896)
    nf = ff // tf
    assert rows % nf == 0
    return pl.pallas_call(
        functools.partial(_expert_body, rows=rows, nf=nf),
        out_shape=jax.ShapeDtypeStruct((n_rows, d), F32),
        grid_spec=pltpu.PrefetchScalarGridSpec(
            num_scalar_prefetch=3,
            grid=(n_rows // rows, nf),
            in_specs=[pl.BlockSpec(memory_space=pl.ANY), _full(g.shape),
                      pl.BlockSpec((None, d, tf), lambda i, f, be, *_: (be[i], 0, f)),
                      pl.BlockSpec((None, d, tf), lambda i, f, be, *_: (be[i], 0, f)),
                      pl.BlockSpec((None, tf, d), lambda i, f, be, *_: (be[i], f, 0))],
            out_specs=pl.BlockSpec((rows, d), lambda i, f, *_: (i, 0)),
            scratch_shapes=[pltpu.VMEM((2, rows, d), F32), pltpu.VMEM((rows, d), BF16),
                            pltpu.VMEM((rows, d), F32), pltpu.SemaphoreType.DMA((2,))]),
        compiler_params=_cparams(("arbitrary", "arbitrary")),
        name="moe_experts",
    )(blk_e, blk_valid, row_tok, x, g, wg, wu, wd)


def _combine_body(dest_ref, x_ref, r_ref, ys_ref, o_ref, buf, sem, *, tm):
    base = pl.program_id(0) * tm

    def copy(r, k):
        return pltpu.make_async_copy(ys_ref.at[pl.ds(dest_ref[(base + r) * TOP_K + k], 1)],
                                     buf.at[k, pl.ds(r, 1)], sem)

    def start(r, c):
        for k in range(TOP_K):
            copy(r, k).start()
        return c

    def wait(r, c):
        for k in range(TOP_K):
            copy(r, k).wait()
        return c

    lax.fori_loop(0, tm, start, 0, unroll=8)
    lax.fori_loop(0, tm, wait, 0, unroll=8)
    o_ref[...] = x_ref[...] + sum(r_ref[:, TOP_K + k:TOP_K + k + 1] * buf[k] for k in range(TOP_K))


def _combine(dest, x, r, ys):
    t, d = x.shape
    tm = _tile(t, 256)
    return pl.pallas_call(
        functools.partial(_combine_body, tm=tm),
        out_shape=jax.ShapeDtypeStruct((t, d), F32),
        grid_spec=pltpu.PrefetchScalarGridSpec(
            num_scalar_prefetch=1,
            grid=(t // tm,),
            in_specs=[pl.BlockSpec((tm, d), lambda i, de: (i, 0)), pl.BlockSpec((tm, LANE), lambda i, de: (i, 0)),
                      pl.BlockSpec(memory_space=pl.ANY)],
            out_specs=pl.BlockSpec((tm, d), lambda i, de: (i, 0)),
            scratch_shapes=[pltpu.VMEM((TOP_K, tm, d), F32), pltpu.SemaphoreType.DMA]),
        compiler_params=_cparams(("arbitrary",)),
        name="moe_combine",
    )(dest, x, r, ys)


def _moe(x, g, w_router, wg, wu, wd):
    t, d = x.shape
    n_assign = t * TOP_K
    rows = MOE_ROWS
    n_blk = -(-(n_assign + N_EXPERTS * (rows - 1)) // rows)
    n_rows = n_blk * rows
    wr_pad = jnp.zeros((d, LANE), F32).at[:, :N_EXPERTS].set(w_router)
    r = _router(x, g, wr_pad)
    top_e = r[:, 0:TOP_K].astype(I32)
    flat_e = top_e.reshape(-1)
    onehot = (flat_e[:, None] == jnp.arange(N_EXPERTS, dtype=I32)[None, :]).astype(I32)
    csum = jnp.cumsum(onehot, axis=0)
    rank = jnp.sum(csum * onehot, axis=1) - 1
    counts = csum[-1]
    padded = (counts + rows - 1) // rows * rows
    pad_end = jnp.cumsum(padded)
    pad_start = pad_end - padded
    dest = (pad_start[flat_e] + rank).astype(I32)
    flat_t = jnp.repeat(jnp.arange(t, dtype=I32), TOP_K)
    row_tok = jnp.zeros((n_rows,), I32).at[dest].set(flat_t)
    blk_start = jnp.arange(n_blk, dtype=I32) * rows
    blk_e = jnp.minimum(jnp.searchsorted(pad_end, blk_start, side="right"), N_EXPERTS - 1).astype(I32)
    blk_valid = (blk_start < pad_end[-1]).astype(I32)
    ys = _experts(blk_e, blk_valid, row_tok, x, g, wg, wu, wd)
    return _combine(dest, x, r, ys)


def _slot_pad(w, n_heads, width):
    lead = w.shape[:-1]
    w = w.reshape(lead + (n_heads, width))
    w = jnp.pad(w, [(0, 0)] * len(lead) + [(0, 0), (0, LANE - width)])
    return w.reshape(lead + (n_heads * LANE,))


def _rope_swap(w):
    half = MLA_ROPE // 2
    z = jnp.zeros(w.shape[:-1] + (MLA_NOPE,), w.dtype)
    tail = jnp.zeros(w.shape[:-1] + (LANE - MLA_QK,), w.dtype)
    return jnp.concatenate([z, w[..., MLA_NOPE + half:MLA_QK], w[..., MLA_NOPE:MLA_NOPE + half], tail], axis=-1)


def _block_diag(w):
    n, a, b = w.shape
    return jnp.einsum("nab,nm->namb", w, jnp.eye(n, dtype=w.dtype)).reshape(n * a, n * b)


def _rel_bucket_by_distance():
    n = np.arange(REL_MAX_DIST)
    exact = REL_BUCKETS // 2
    log_ratio = np.log(np.maximum(n, exact).astype(np.float32) / exact) / math.log(REL_MAX_DIST / exact)
    large = np.minimum(exact + (log_ratio * (REL_BUCKETS - exact)).astype(np.int32), REL_BUCKETS - 1)
    return np.where(n < exact, n, large).astype(np.int32)


def kernel(x, mem, positions, rel_table, g_mix, g_cross, g_mem, g_ffn, w_in, b_gate, s5_lam_re, s5_lam_im, s5_log_step, s5_b_re, s5_b_im, s5_c_re, s5_c_im, s5_d, s5_w_glu, s5_b_glu, mla_g_cq, mla_g_ckv, mla_w_uq, mla_w_ukv, mla_g_qn, mla_g_kn, lru_conv_w, lru_conv_b, lru_w_r, lru_b_r, lru_w_i, lru_b_i, lru_lam, diff_g_qn, diff_g_kn, diff_lq1, diff_lk1, diff_lq2, diff_lk2, diff_g_sub, w_branch, w_out, x_wq, x_wk, x_wv, x_wo, x_g_qn, x_g_kn, ffn_w_gate, ffn_w_up, ffn_w_down, moe_w_router, moe_w_gate, moe_w_up, moe_w_down):
    b, s, d = x.shape
    t = b * s
    depth = w_in.shape[0]
    positions = positions.astype(I32)
    pos_col = positions.reshape(t, 1)
    assert b <= SUB, "the recurrent kernel packs one time step of all sequences into at most one vreg row-group"
    slots = 1 << (b - 1).bit_length()
    row2 = lambda a: a.reshape(1, -1).astype(F32)
    bd32 = jnp.asarray(np.kron(np.eye(DIFF_W // DIFF_D), np.ones((DIFF_D, DIFF_D))), F32)
    bd64 = jnp.asarray(np.kron(np.eye(CROSS_H), np.ones((CROSS_D, CROSS_D))), F32)
    half = MLA_ROPE // 2
    inv = ROPE_THETA ** (-np.arange(half, dtype=np.float32) / half)
    inv_full = np.zeros((1, LANE), np.float32)
    inv_full[0, MLA_NOPE:MLA_NOPE + half] = inv
    inv_full[0, MLA_NOPE + half:MLA_QK] = inv
    sgn = np.zeros((1, LANE), np.float32)
    sgn[0, MLA_NOPE:MLA_NOPE + half] = -1.0
    sgn[0, MLA_NOPE + half:MLA_QK] = 1.0
    bucket = _rel_bucket_by_distance()

    xf = x.reshape(t, d)
    for l in range(depth):
        lam_init = 0.8 - 0.6 * math.exp(-0.3 * l)
        w = w_in[l]
        off = np.cumsum([0, S5_W, MLA_QR, MLA_KVR, MLA_ROPE, LRU_W, LRU_W, DIFF_W, DIFF_W, DIFF_W])
        c_s5, c_cq, c_ckv, c_kpe, c_lx, c_lg, c_qd, c_kd, c_vd = (w[:, off[i]:off[i + 1]] for i in range(9))
        c_gate = w[:, off[9]:]
        zpad = lambda n: jnp.zeros((d, n), F32)
        kpe_pos = jnp.concatenate([zpad(MLA_NOPE), c_kpe, zpad(LANE - MLA_QK)], axis=1)
        kpe_sw = jnp.concatenate([zpad(MLA_NOPE), c_kpe[:, half:], c_kpe[:, :half], zpad(LANE - MLA_QK)], axis=1)
        w_rec = jnp.concatenate([c_s5, c_lx, c_lg], axis=1).astype(BF16)
        w_mla = jnp.concatenate([c_cq, zpad(2 * LANE - MLA_QR), c_ckv, kpe_pos, kpe_sw], axis=1).astype(BF16)
        w_diff = jnp.concatenate([c_qd, c_kd, c_vd], axis=1).astype(BF16)
        z_rec, z_mla, z_diff = _inproj(xf, row2(g_mix[l]), w_rec, w_mla, w_diff)

        dt = jnp.exp(s5_log_step[l])[:, None]
        lr, li = s5_lam_re[l], s5_lam_im[l]
        mag = jnp.exp(lr * dt)
        ar, ai = mag * jnp.cos(li * dt), mag * jnp.sin(li * dt)
        den = lr * lr + li * li
        zr = ((ar - 1.0) * lr + ai * li) / den
        zi = (ai * lr - (ar - 1.0) * li) / den
        bbr = zr[..., None] * s5_b_re[l] - zi[..., None] * s5_b_im[l]
        bbi = zr[..., None] * s5_b_im[l] + zi[..., None] * s5_b_re[l]
        rec_p = dict(
            ar=row2(ar), ai=row2(ai),
            bre=_block_diag(jnp.swapaxes(bbr, 1, 2)).astype(BF16), bim=_block_diag(jnp.swapaxes(bbi, 1, 2)).astype(BF16),
            cre=_block_diag(jnp.swapaxes(s5_c_re[l], 1, 2)).astype(BF16),
            cim=_block_diag(jnp.swapaxes(-s5_c_im[l], 1, 2)).astype(BF16),
            d=row2(s5_d[l]), wglu=s5_w_glu[l].astype(BF16), bglu=row2(s5_b_glu[l]),
            cw=lru_conv_w[l].astype(F32), cb=row2(lru_conv_b[l]),
            wr=_block_diag(lru_w_r[l]).astype(BF16), br=row2(lru_b_r[l]),
            wi=_block_diag(lru_w_i[l]).astype(BF16), bi=row2(lru_b_i[l]), lam=row2(lru_lam[l]))
        zt = jnp.swapaxes(z_rec.reshape(b, s, -1), 0, 1)
        zt = jnp.pad(zt, ((0, 0), (0, slots - b), (0, 0))).reshape(s * slots, -1)
        y_rec = _recurrent(zt, rec_p, slots).reshape(s, slots, -1)[:, :b]
        y_rec = jnp.swapaxes(y_rec, 0, 1).reshape(t, -1)

        wuq = mla_w_uq[l].reshape(MLA_QR, MLA_H, MLA_QK)
        wukv = mla_w_ukv[l].reshape(MLA_KVR, MLA_H, MLA_NOPE + MLA_V)
        rows_pad = lambda a: jnp.pad(a, ((0, 2 * LANE - MLA_QR), (0, 0)))
        mla_p = dict(
            gcq=jnp.pad(row2(mla_g_cq[l]), ((0, 0), (0, 2 * LANE - MLA_QR))), gckv=row2(mla_g_ckv[l]),
            wq=rows_pad(_slot_pad(mla_w_uq[l], MLA_H, MLA_QK)).astype(BF16),
            wqs=rows_pad(_rope_swap(wuq).reshape(MLA_QR, MLA_H * LANE)).astype(BF16),
            wk=_slot_pad(wukv[..., :MLA_NOPE].reshape(MLA_KVR, -1), MLA_H, MLA_NOPE).astype(BF16),
            wv=wukv[..., MLA_NOPE:].reshape(MLA_KVR, -1).astype(BF16),
            gq=_slot_pad(row2(mla_g_qn[l]), 1, MLA_QK), gqs=_rope_swap(row2(mla_g_qn[l])),
            gk=_slot_pad(row2(mla_g_kn[l]), 1, MLA_QK), gks=_rope_swap(row2(mla_g_kn[l])),
            inv=jnp.asarray(inv_full), sgn=jnp.asarray(sgn))
        q_m, k_m, v_m = _mla_prep(z_mla, pos_col, mla_p, _tile(s, 512))
        y_mla = _mla_attn(q_m, k_m, v_m, positions)

        diff_p = dict(
            gq=jnp.tile(row2(diff_g_qn[l]), (1, DIFF_W // DIFF_D)) * DIFF_D ** -0.5,
            gk=jnp.tile(row2(diff_g_kn[l]), (1, DIFF_W // DIFF_D)), bd32=bd32, bd64=bd64,
            tab=jnp.transpose(rel_table[bucket]).astype(F32),
            lqk=jnp.pad(jnp.stack([diff_lq1[l], diff_lk1[l], diff_lq2[l], diff_lk2[l]]).astype(F32),
                        ((0, SUB - 4), (0, LANE - DIFF_D))),
            gsub=jnp.tile(row2(diff_g_sub[l]), (1, DIFF_H)) * (1.0 - lam_init))
        q_d, k_d, v_d = _diff_prep(z_diff, diff_p, _tile(s, 256))
        y_diff = _diff_attn(q_d, k_d, v_d, positions, diff_p, lam_init)

        cross_p = dict(gmem=row2(g_mem[l]), wk=x_wk[l].astype(BF16), wv=x_wv[l].astype(BF16),
                       gk=jnp.tile(row2(x_g_kn[l]), (1, CROSS_H)), bd64=bd64)
        kx, vx = _cross_kv(mem, cross_p)
        merge_p = dict(
            gmix=row2(g_mix[l]), wg=c_gate.astype(BF16), bg=row2(b_gate[l]),
            ps5=w_branch[l, 0].astype(BF16),
            pmla=w_branch[l, 1].astype(BF16),
            plru=w_branch[l, 2].astype(BF16), pdiff=w_branch[l, 3].astype(BF16), wout=w_out[l].astype(BF16),
            gcross=row2(g_cross[l]), wq=x_wq[l].astype(BF16),
            gq=jnp.tile(row2(x_g_qn[l]), (1, CROSS_H)) * CROSS_D ** -0.5, bd64=bd64, wo=x_wo[l].astype(BF16))
        xf = _merge(xf, y_rec, y_mla, y_diff, kx, vx, merge_p, s)

        if l % 2 == 0:
            e = l // 2
            xf = _ffn(xf, row2(g_ffn[l]), ffn_w_gate[e].astype(BF16), ffn_w_up[e].astype(BF16),
                      ffn_w_down[e].astype(BF16))
        else:
            e = l // 2
            xf = _moe(xf, row2(g_ffn[l]), moe_w_router[e], moe_w_gate[e].astype(BF16),
                      moe_w_up[e].astype(BF16), moe_w_down[e].astype(BF16))
    return xf.reshape(b, s, d)
```
